```python
import math
import jax, jax.numpy as jnp
from jax import lax
import numpy as np

D_MODEL = 1024
BATCH = 4
SEQ = 8192
DEPTH = 1

RET_HEADS = 4
RET_HEAD_DIM = 128
RET_WIDTH = RET_HEADS * RET_HEAD_DIM
RET_CHUNK = 256
MOBA_HEADS = 8
MOBA_HEAD_DIM = 64
MOBA_WIDTH = MOBA_HEADS * MOBA_HEAD_DIM
MOBA_BLOCK = 256
MOBA_TOP_K = 3
MOBA_Q_CHUNK = 64
MIX_WIDTH = RET_WIDTH + MOBA_WIDTH
IN_WIDTH = 4 * RET_WIDTH + 3 * MOBA_WIDTH
D_FF = ((8 * D_MODEL + 3 * 256 - 1) // (3 * 256)) * 256
DEEPNORM_ALPHA = (2 * DEPTH) ** 0.25
DEEPNORM_BETA = (8 * DEPTH) ** -0.25
NORM_EPS = 1e-5
NEG_INF = -1e30
SEQ_PAD_MULTIPLE = math.lcm(RET_CHUNK, MOBA_BLOCK)

kernel_name = "hybrid_retention_moba_deepnorm"


def _layer_norm(x, g, b):
    xf = x.astype(jnp.float32)
    mu = jnp.mean(xf, axis=-1, keepdims=True)
    var = jnp.mean(jnp.square(xf - mu), axis=-1, keepdims=True)
    return ((xf - mu) * lax.rsqrt(var + NORM_EPS) * g + b).astype(x.dtype)


def _split_in_proj(p):
    sizes = [RET_WIDTH, RET_WIDTH, RET_WIDTH, RET_WIDTH, MOBA_WIDTH, MOBA_WIDTH, MOBA_WIDTH]
    offs = np.cumsum([0] + sizes)
    return [p[..., int(offs[i]):int(offs[i + 1])] for i in range(len(sizes))]


def _retention(q, k, v):
    f32 = jnp.float32
    bsz, s_len, n_h, d_k = q.shape
    d_v = v.shape[-1]
    n_c = s_len // RET_CHUNK
    q = q.astype(f32).reshape(bsz, n_c, RET_CHUNK, n_h, d_k)
    k = (k.astype(f32) * d_k ** -0.5).reshape(bsz, n_c, RET_CHUNK, n_h, d_k)
    v = v.astype(f32).reshape(bsz, n_c, RET_CHUNK, n_h, d_v)
    log_g = jnp.log1p(-jnp.exp2(-5.0 - jnp.arange(n_h, dtype=f32)))
    pos = jnp.arange(RET_CHUNK, dtype=f32)
    diff = pos[:, None] - pos[None, :]
    intra_decay = jnp.where(diff >= 0.0,
                            jnp.exp(log_g[:, None, None] * jnp.maximum(diff, 0.0)), 0.0)
    scores = jnp.einsum('bnihd,bnjhd->bnhij', q, k) * intra_decay
    intra = jnp.einsum('bnhij,bnjhe->bnihe', scores, v)
    k_to_end = jnp.exp(log_g[None, :] * (RET_CHUNK - 1.0 - pos)[:, None])
    kv = jnp.einsum('bnjhd,bnjhe->nbhde', k * k_to_end[:, :, None], v)
    chunk_decay = jnp.exp(log_g * RET_CHUNK)[None, :, None, None]

    def step(state, kv_c):
        return state * chunk_decay + kv_c, state

    _, prev = lax.scan(step, jnp.zeros_like(kv[0]), kv)
    q_from_start = jnp.exp(log_g[None, :] * (pos + 1.0)[:, None])
    cross = jnp.einsum('bnihd,nbhde->bnihe', q * q_from_start[:, :, None], prev)
    return (intra + cross).reshape(bsz, s_len, n_h, d_v)


def _head_group_norm(y, gain):
    bsz, s_len, n_h, e = y.shape
    mu = jnp.mean(y, axis=-1, keepdims=True)
    var = jnp.mean(jnp.square(y - mu), axis=-1, keepdims=True)
    yn = (y - mu) * lax.rsqrt(var + NORM_EPS)
    return yn.reshape(bsz, s_len, n_h * e) * gain


def _alibi_slopes(n_heads):
    return jnp.exp2(-8.0 * (jnp.arange(n_heads, dtype=jnp.float32) + 1.0) / n_heads)


def _moba_attention(q, k, v):
    f32 = jnp.float32
    bsz, s_len, n_h, d_h = q.shape
    n_b = s_len // MOBA_BLOCK
    top_k = min(MOBA_TOP_K, n_b)
    q = jnp.transpose(q, (0, 2, 1, 3)).astype(f32) * d_h ** -0.5
    k = jnp.transpose(k, (0, 2, 1, 3)).astype(f32)
    v = jnp.transpose(v, (0, 2, 1, 3)).astype(f32)
    kb = k.reshape(bsz, n_h, n_b, MOBA_BLOCK, d_h)
    vb = v.reshape(bsz, n_h, n_b, MOBA_BLOCK, d_h)
    k_mean = jnp.mean(kb, axis=3)
    slopes = _alibi_slopes(n_h)
    b_idx = jnp.arange(bsz)[:, None, None, None]
    h_idx = jnp.arange(n_h)[None, :, None, None]
    blk_ids = jnp.arange(n_b)
    offs = jnp.arange(MOBA_BLOCK)

    def one_chunk(start):
        qc = lax.dynamic_slice_in_dim(q, start, MOBA_Q_CHUNK, axis=2)
        t_q = start + jnp.arange(MOBA_Q_CHUNK)
        own = start // MOBA_BLOCK
        gate = jnp.einsum('bhqd,bhnd->bhqn', qc, k_mean)
        gate = jnp.where(blk_ids < own, gate, NEG_INF)
        _, sel = lax.top_k(gate, top_k)
        sel_valid = jnp.arange(top_k) < own
        k_sel = kb[b_idx, h_idx, sel]
        v_sel = vb[b_idx, h_idx, sel]
        dist_sel = (t_q[None, None, :, None, None]
                    - (sel[..., None] * MOBA_BLOCK + offs)).astype(f32)
        s_sel = (jnp.einsum('bhqd,bhqkjd->bhqkj', qc, k_sel)
                 - slopes[:, None, None, None] * dist_sel)
        s_sel = jnp.where(sel_valid[:, None], s_sel, NEG_INF)
        k_own = lax.dynamic_slice_in_dim(k, own * MOBA_BLOCK, MOBA_BLOCK, axis=2)
        v_own = lax.dynamic_slice_in_dim(v, own * MOBA_BLOCK, MOBA_BLOCK, axis=2)
        dist_own = t_q[:, None] - (own * MOBA_BLOCK + offs)[None, :]
        s_own = (jnp.einsum('bhqd,bhjd->bhqj', qc, k_own)
                 - slopes[:, None, None] * dist_own.astype(f32))
        s_own = jnp.where(dist_own >= 0, s_own, NEG_INF)
        logits = jnp.concatenate(
            [s_sel.reshape(bsz, n_h, MOBA_Q_CHUNK, top_k * MOBA_BLOCK), s_own], axis=-1)
        p = jax.nn.softmax(logits, axis=-1)
        p_sel = p[..., :top_k * MOBA_BLOCK].reshape(bsz, n_h, MOBA_Q_CHUNK, top_k, MOBA_BLOCK)
        p_own = p[..., top_k * MOBA_BLOCK:]
        return (jnp.einsum('bhqkj,bhqkjd->bhqd', p_sel, v_sel)
                + jnp.einsum('bhqj,bhjd->bhqd', p_own, v_own))

    starts = jnp.arange(0, s_len, MOBA_Q_CHUNK)
    out = lax.map(one_chunk, starts)
    return jnp.transpose(out, (1, 0, 3, 2, 4)).reshape(bsz, s_len, n_h, d_h)


def setup_inputs(seed: int = 0) -> dict:
    key = jax.random.key(seed)
    ks = jax.random.split(key, 12)
    f32 = jnp.float32
    nrm = lambda k_, shp: jax.random.normal(k_, shp, f32)
    col_scale = jnp.concatenate([
        jnp.ones((2 * RET_WIDTH,), f32), jnp.full((RET_WIDTH,), DEEPNORM_BETA, f32),
        jnp.ones((RET_WIDTH,), f32), jnp.ones((2 * MOBA_WIDTH,), f32),
        jnp.full((MOBA_WIDTH,), DEEPNORM_BETA, f32)])
    x = nrm(ks[0], (BATCH, SEQ, D_MODEL))
    w_in = nrm(ks[1], (DEPTH, D_MODEL, IN_WIDTH)) * (D_MODEL ** -0.5) * col_scale
    ret_gn_gain = 1.0 + 0.02 * nrm(ks[2], (DEPTH, RET_WIDTH))
    w_out = nrm(ks[3], (DEPTH, MIX_WIDTH, D_MODEL)) * (MIX_WIDTH ** -0.5) * DEEPNORM_BETA
    ln1_g = 1.0 + 0.02 * nrm(ks[4], (DEPTH, D_MODEL))
    ln1_b = 0.02 * nrm(ks[5], (DEPTH, D_MODEL))
    w_gate = nrm(ks[6], (DEPTH, D_MODEL, D_FF)) * (D_MODEL ** -0.5) * DEEPNORM_BETA
    w_up = nrm(ks[7], (DEPTH, D_MODEL, D_FF)) * (D_MODEL ** -0.5) * DEEPNORM_BETA
    w_down = nrm(ks[8], (DEPTH, D_FF, D_MODEL)) * (D_FF ** -0.5) * DEEPNORM_BETA
    ln2_g = 1.0 + 0.02 * nrm(ks[9], (DEPTH, D_MODEL))
    ln2_b = 0.02 * nrm(ks[10], (DEPTH, D_MODEL))
    return {"x": x, "w_in": w_in, "ret_gn_gain": ret_gn_gain, "w_out": w_out,
            "ln1_g": ln1_g, "ln1_b": ln1_b, "w_gate": w_gate, "w_up": w_up,
            "w_down": w_down, "ln2_g": ln2_g, "ln2_b": ln2_b}


def reference(x, w_in, ret_gn_gain, w_out, ln1_g, ln1_b, w_gate, w_up, w_down, ln2_g, ln2_b):
    bsz, s_len, _ = x.shape
    s_pad = -(-s_len // SEQ_PAD_MULTIPLE) * SEQ_PAD_MULTIPLE
    h = x
    for layer in range(DEPTH):
        proj = h @ w_in[layer]
        proj_p = jnp.pad(proj, ((0, 0), (0, s_pad - s_len), (0, 0)))
        rq, rk, rv, rg, mq, mk, mv = _split_in_proj(proj_p)
        ret = _retention(rq.reshape(bsz, s_pad, RET_HEADS, RET_HEAD_DIM),
                         rk.reshape(bsz, s_pad, RET_HEADS, RET_HEAD_DIM),
                         rv.reshape(bsz, s_pad, RET_HEADS, RET_HEAD_DIM))[:, :s_len]
        ret = _head_group_norm(ret, ret_gn_gain[layer])
        ret = jax.nn.silu(rg[:, :s_len].astype(jnp.float32)) * ret
        moba = _moba_attention(mq.reshape(bsz, s_pad, MOBA_HEADS, MOBA_HEAD_DIM),
                               mk.reshape(bsz, s_pad, MOBA_HEADS, MOBA_HEAD_DIM),
                               mv.reshape(bsz, s_pad, MOBA_HEADS, MOBA_HEAD_DIM))[:, :s_len]
        moba = moba.reshape(bsz, s_len, MOBA_WIDTH)
        mixed = jnp.concatenate([ret, moba], axis=-1).astype(h.dtype) @ w_out[layer]
        h = _layer_norm(DEEPNORM_ALPHA * h + mixed, ln1_g[layer], ln1_b[layer])
        ffn = (jax.nn.silu(h @ w_gate[layer]) * (h @ w_up[layer])) @ w_down[layer]
        h = _layer_norm(DEEPNORM_ALPHA * h + ffn, ln2_g[layer], ln2_b[layer])
    return h
```

```python
import functools
import math

import numpy as np
import jax
import jax.numpy as jnp
from jax import lax
from jax.experimental import pallas as pl
from jax.experimental.pallas import tpu as pltpu

F32 = jnp.float32
BF16 = jnp.bfloat16

RET_HEADS = 4
RET_HEAD_DIM = 128
RET_WIDTH = RET_HEADS * RET_HEAD_DIM
RET_CHUNK = 256
MOBA_HEADS = 8
MOBA_HEAD_DIM = 64
MOBA_WIDTH = MOBA_HEADS * MOBA_HEAD_DIM
MOBA_BLOCK = 256
MOBA_TOP_K = 3
NORM_EPS = 1e-5
NEG_INF = -1e30

V7X_VMEM_BYTES = 64 * 1024 * 1024
LANES = 128
TOKEN_TILE = 512


def _vmem_limit(estimate_bytes):
    return int(min(V7X_VMEM_BYTES - 4 * 1024 * 1024, max(estimate_bytes, 16 * 1024 * 1024)))


def _nt_dot(a, b):
    return lax.dot_general(a, b, (((1,), (1,)), ((), ())), preferred_element_type=F32)


def _layer_norm(v, g, b):
    mu = jnp.mean(v, axis=-1, keepdims=True)
    d = v - mu
    var = jnp.mean(d * d, axis=-1, keepdims=True)
    return d * lax.rsqrt(var + NORM_EPS) * g + b


def _inproj_kernel(x_ref, wn_ref, wt_ref, rqkv_ref, rg_ref, mk_ref, mqt_ref, mvt_ref):
    x = x_ref[0].astype(BF16)
    nat = jnp.dot(x, wn_ref[...], preferred_element_type=F32)
    rqkv_ref[0] = nat[:, :3 * RET_WIDTH].astype(BF16)
    rg_ref[0] = nat[:, 3 * RET_WIDTH:4 * RET_WIDTH]
    mk_ref[0] = nat[:, 4 * RET_WIDTH:].astype(BF16)
    tr = _nt_dot(wt_ref[...], x)
    for c in range(x.shape[0] // MOBA_BLOCK):
        cols = slice(c * MOBA_BLOCK, (c + 1) * MOBA_BLOCK)
        mqt_ref[0, c] = (tr[:MOBA_WIDTH, cols] * MOBA_HEAD_DIM ** -0.5).astype(BF16)
        mvt_ref[0, c] = tr[MOBA_WIDTH:, cols].astype(BF16)


def _inproj(x, w_nat, w_tr):
    bsz, s_len, d = x.shape
    tm = TOKEN_TILE
    nb = s_len // MOBA_BLOCK
    n_nat = w_nat.shape[1]
    est = (2 * tm * d * 4 + 2 * (w_nat.size + w_tr.size) * 2
           + 2 * tm * (3 * RET_WIDTH * 2 + RET_WIDTH * 4 + 3 * MOBA_WIDTH * 2)
           + tm * (n_nat + 2 * MOBA_WIDTH) * 4 * 2)
    return pl.pallas_call(
        _inproj_kernel,
        grid=(bsz, s_len // tm),
        in_specs=[
            pl.BlockSpec((1, tm, d), lambda b, t: (b, t, 0)),
            pl.BlockSpec((d, n_nat), lambda b, t: (0, 0)),
            pl.BlockSpec((2 * MOBA_WIDTH, d), lambda b, t: (0, 0)),
        ],
        out_specs=[
            pl.BlockSpec((1, tm, 3 * RET_WIDTH), lambda b, t: (b, t, 0)),
            pl.BlockSpec((1, tm, RET_WIDTH), lambda b, t: (b, t, 0)),
            pl.BlockSpec((1, tm, MOBA_WIDTH), lambda b, t: (b, t, 0)),
            pl.BlockSpec((1, tm // MOBA_BLOCK, MOBA_WIDTH, MOBA_BLOCK), lambda b, t: (b, t, 0, 0)),
            pl.BlockSpec((1, tm // MOBA_BLOCK, MOBA_WIDTH, MOBA_BLOCK), lambda b, t: (b, t, 0, 0)),
        ],
        out_shape=[
            jax.ShapeDtypeStruct((bsz, s_len, 3 * RET_WIDTH), BF16),
            jax.ShapeDtypeStruct((bsz, s_len, RET_WIDTH), F32),
            jax.ShapeDtypeStruct((bsz, s_len, MOBA_WIDTH), BF16),
            jax.ShapeDtypeStruct((bsz, nb, MOBA_WIDTH, MOBA_BLOCK), BF16),
            jax.ShapeDtypeStruct((bsz, nb, MOBA_WIDTH, MOBA_BLOCK), BF16),
        ],
        compiler_params=pltpu.CompilerParams(
            dimension_semantics=("parallel", "parallel"),
            vmem_limit_bytes=_vmem_limit(est)),
        name="inproj",
    )(x, w_nat, w_tr)


def _retention_tables():
    h = np.arange(RET_HEADS, dtype=np.float64)
    log_g = np.log1p(-np.exp2(-5.0 - h))
    pos = np.arange(RET_CHUNK, dtype=np.float64)
    diff = pos[:, None] - pos[None, :]
    scale = RET_HEAD_DIM ** -0.5
    intra = np.where(diff >= 0.0, np.exp(log_g[:, None, None] * np.maximum(diff, 0.0)), 0.0) * scale
    k_to_end = np.exp(log_g[:, None] * (RET_CHUNK - 1.0 - pos)[None, :]) * scale
    q_from_start = np.exp(log_g[:, None] * (pos + 1.0)[None, :])
    chunk_decay = np.exp(log_g * RET_CHUNK)
    bcast = lambda t: np.broadcast_to(t[:, :, None], (RET_HEADS, RET_CHUNK, RET_HEAD_DIM))
    return (jnp.asarray(intra, F32), jnp.asarray(bcast(k_to_end), F32),
            jnp.asarray(bcast(q_from_start), F32), jnp.asarray(chunk_decay, F32))


def _retention_kernel(cd_ref, q_ref, k_ref, v_ref, rg_ref, gain_ref, dec_ref, kte_ref, qfs_ref,
                      o_ref, state_ref):
    h = pl.program_id(1)

    @pl.when(pl.program_id(2) == 0)
    def _():
        state_ref[...] = jnp.zeros_like(state_ref)

    q = q_ref[0]
    k = k_ref[0]
    v = v_ref[0]
    scores = _nt_dot(q, k) * dec_ref[0]
    intra = jnp.dot(scores.astype(BF16), v, preferred_element_type=F32)
    prev = state_ref[...]
    cross = jnp.dot(q, prev.astype(BF16), preferred_element_type=F32) * qfs_ref[0]
    y = intra + cross
    k_dec = (k.astype(F32) * kte_ref[0]).T.astype(BF16)
    kv = jnp.dot(k_dec, v, preferred_element_type=F32)
    state_ref[...] = prev * cd_ref[h] + kv
    mu = jnp.mean(y, axis=-1, keepdims=True)
    d = y - mu
    var = jnp.mean(d * d, axis=-1, keepdims=True)
    yn = d * lax.rsqrt(var + NORM_EPS) * gain_ref[...]
    g = rg_ref[0]
    o_ref[0] = (g * jax.nn.sigmoid(g) * yn).astype(o_ref.dtype)


def _retention(rqkv, rg, gain, tables):
    bsz, s_len, _ = rqkv.shape
    dec, kte, qfs, cd = tables
    c, e = RET_CHUNK, RET_HEAD_DIM
    tab_spec = lambda shape: pl.BlockSpec((1,) + shape, lambda b, h, n: (h, 0, 0))
    return pl.pallas_call(
        _retention_kernel,
        grid=(bsz, RET_HEADS, s_len // c),
        in_specs=[
            pl.BlockSpec(memory_space=pltpu.SMEM),
            pl.BlockSpec((1, c, e), lambda b, h, n: (b, n, h)),
            pl.BlockSpec((1, c, e), lambda b, h, n: (b, n, RET_HEADS + h)),
            pl.BlockSpec((1, c, e), lambda b, h, n: (b, n, 2 * RET_HEADS + h)),
            pl.BlockSpec((1, c, e), lambda b, h, n: (b, n, h)),
            pl.BlockSpec((1, e), lambda b, h, n: (0, h)),
            tab_spec((c, c)), tab_spec((c, e)), tab_spec((c, e)),
        ],
        out_specs=pl.BlockSpec((1, c, e), lambda b, h, n: (b, n, h)),
        out_shape=jax.ShapeDtypeStruct((bsz, s_len, RET_WIDTH), BF16),
        scratch_shapes=[pltpu.VMEM((e, e), F32)],
        compiler_params=pltpu.CompilerParams(
            dimension_semantics=("parallel", "parallel", "arbitrary")),
        name="retention",
    )(cd, rqkv, rqkv, rqkv, rg, gain, dec, kte, qfs)


def _alibi_tables():
    slopes = np.exp2(-8.0 * (np.arange(MOBA_HEADS, dtype=np.float64) + 1.0) / MOBA_HEADS)
    r = np.arange(MOBA_BLOCK, dtype=np.float64)
    tab = -slopes[:, None, None] * (r[None, None, :] - r[None, :, None])
    return jnp.asarray(tab, F32), jnp.asarray(slopes, F32)


def _moba_kernel(slope_ref, qt_ref, k_ref, vt_ref, tab_ref, o_ref, kmean_ref, bias_ref):
    blk, hd = MOBA_BLOCK, MOBA_HEAD_DIM
    n_b = vt_ref.shape[1]
    pair = pl.program_id(1)
    i = pl.program_id(2)

    @pl.when(i == 0)
    def _():
        for b in range(n_b):
            kb = k_ref[0, b * blk:(b + 1) * blk, :].astype(F32)
            kmean_ref[b:b + 1, :] = jnp.sum(kb, axis=0, keepdims=True) * (1.0 / blk)

    qt = qt_ref[0, 0]
    row = lax.broadcasted_iota(jnp.int32, qt.shape, 0)
    kmean = kmean_ref[...].astype(BF16)
    jidx = lax.broadcasted_iota(jnp.int32, (n_b, blk), 0)
    k_pos = lax.broadcasted_iota(jnp.int32, (blk, blk), 0)
    q_pos = lax.broadcasted_iota(jnp.int32, (blk, blk), 1)
    outs = []
    for hl in range(2):
        slope = slope_ref[2 * pair + hl]
        qh = jnp.where((row >= hl * hd) & (row < (hl + 1) * hd), qt, jnp.zeros_like(qt))
        gate = jnp.dot(kmean, qh, preferred_element_type=F32)
        cnt = jnp.zeros((n_b, blk), jnp.int32)
        for jp in range(n_b):
            gj = gate[jp:jp + 1, :]
            beats = (gj > gate) | ((gj == gate) & (jp < jidx))
            cnt = cnt + jnp.where(beats & (jp < i), 1, 0)
        sel = (cnt < MOBA_TOP_K) & (jidx < i)
        block_off = (i - jidx).astype(F32) * (-float(blk) * slope)
        bias_ref[hl] = jnp.where(sel, block_off, NEG_INF)
        tab = tab_ref[hl]

        kd = k_ref[0, pl.ds(pl.multiple_of(i * blk, blk), blk), :]
        u = jnp.where(k_pos <= q_pos, jnp.dot(kd, qh, preferred_element_type=F32) + tab, NEG_INF)
        m = jnp.max(u, axis=0, keepdims=True)
        p = jnp.exp(u - m)
        l = jnp.sum(p, axis=0, keepdims=True)
        vt = vt_ref[0, i, hl * hd:(hl + 1) * hd, :]
        acc = jnp.dot(vt, p.astype(BF16), preferred_element_type=F32)

        def body(j, carry, hl=hl, qh=qh, tab=tab):
            m, l, acc = carry
            kj = k_ref[0, pl.ds(pl.multiple_of(j * blk, blk), blk), :]
            u = jnp.dot(kj, qh, preferred_element_type=F32) + tab
            b = bias_ref[hl, pl.ds(j, 1), :]
            m_new = jnp.maximum(m, jnp.max(u, axis=0, keepdims=True) + b)
            alpha = jnp.exp(m - m_new)
            p = jnp.exp(u + (b - m_new))
            l = alpha * l + jnp.sum(p, axis=0, keepdims=True)
            vt = vt_ref[0, j, hl * hd:(hl + 1) * hd, :]
            acc = alpha * acc + jnp.dot(vt, p.astype(BF16), preferred_element_type=F32)
            return m_new, l, acc

        m, l, acc = lax.fori_loop(0, i, body, (m, l, acc))
        outs.append(acc * (1.0 / l))
    o_ref[0] = jnp.concatenate(outs, axis=0).T.astype(o_ref.dtype)


def _moba(mqt, mk, mvt, tab, slopes):
    bsz, n_b, width, blk = mqt.shape
    s_len = mk.shape[1]
    pw = 2 * MOBA_HEAD_DIM
    est = 2 * (2 * s_len * pw * 2) + 4 * 2 * blk * blk * 4 + 24 * blk * blk * 4
    return pl.pallas_call(
        _moba_kernel,
        grid=(bsz, width // pw, n_b),
        in_specs=[
            pl.BlockSpec(memory_space=pltpu.SMEM),
            pl.BlockSpec((1, 1, pw, blk), lambda b, p, i: (b, i, p, 0)),
            pl.BlockSpec((1, s_len, pw), lambda b, p, i: (b, 0, p)),
            pl.BlockSpec((1, n_b, pw, blk), lambda b, p, i: (b, 0, p, 0)),
            pl.BlockSpec((2, blk, blk), lambda b, p, i: (p, 0, 0)),
        ],
        out_specs=pl.BlockSpec((1, blk, pw), lambda b, p, i: (b, i, p)),
        out_shape=jax.ShapeDtypeStruct((bsz, s_len, width), BF16),
        scratch_shapes=[pltpu.VMEM((n_b, pw), F32), pltpu.VMEM((2, n_b, blk), F32)],
        compiler_params=pltpu.CompilerParams(
            dimension_semantics=("parallel", "parallel", "arbitrary"),
            vmem_limit_bytes=_vmem_limit(est)),
        name="moba",
    )(slopes, mqt, mk, mvt, tab)


def _outproj_kernel(alpha, x_ref, ret_ref, moba_ref, wr_ref, wm_ref, g_ref, b_ref, o_ref):
    mixed = (jnp.dot(ret_ref[0], wr_ref[...], preferred_element_type=F32)
             + jnp.dot(moba_ref[0], wm_ref[...], preferred_element_type=F32))
    o_ref[0] = _layer_norm(alpha * x_ref[0] + mixed, g_ref[...], b_ref[...])


def _outproj(x, ret, moba, w_ret, w_moba, g, b, alpha):
    bsz, s_len, d = x.shape
    tm = TOKEN_TILE
    row = lambda w: pl.BlockSpec((1, tm, w), lambda b_, t: (b_, t, 0))
    full = lambda a: pl.BlockSpec(a.shape, lambda b_, t: (0, 0))
    est = 2 * (2 * tm * d * 4 + 2 * tm * RET_WIDTH * 2 + 2 * d * d * 2) + 4 * tm * d * 4
    return pl.pallas_call(
        functools.partial(_outproj_kernel, alpha),
        grid=(bsz, s_len // tm),
        in_specs=[row(d), row(RET_WIDTH), row(MOBA_WIDTH), full(w_ret), full(w_moba), full(g), full(b)],
        out_specs=row(d),
        out_shape=jax.ShapeDtypeStruct((bsz, s_len, d), F32),
        compiler_params=pltpu.CompilerParams(
            dimension_semantics=("parallel", "parallel"),
            vmem_limit_bytes=_vmem_limit(est)),
        name="outproj_ln1",
    )(x, ret, moba, w_ret, w_moba, g, b)


def _ffn_kernel(alpha, h_ref, wg_ref, wu_ref, wd_ref, g_ref, b_ref, o_ref):
    h = h_ref[0]
    hb = h.astype(BF16)
    gate = jnp.dot(hb, wg_ref[...], preferred_element_type=F32)
    up = jnp.dot(hb, wu_ref[...], preferred_element_type=F32)
    act = (gate * jax.nn.sigmoid(gate) * up).astype(BF16)
    ffn = jnp.dot(act, wd_ref[...], preferred_element_type=F32)
    o_ref[0] = _layer_norm(alpha * h + ffn, g_ref[...], b_ref[...])


def _ffn(h, w_gate, w_up, w_down, g, b, alpha):
    bsz, s_len, d = h.shape
    d_ff = w_gate.shape[1]
    tm = TOKEN_TILE
    row = pl.BlockSpec((1, tm, d), lambda b_, t: (b_, t, 0))
    full = lambda a: pl.BlockSpec(a.shape, lambda b_, t: (0, 0), pipeline_mode=pl.Buffered(1))
    est = 3 * d * d_ff * 2 + 4 * tm * d * 4 + tm * d_ff * (4 + 4 + 2) + 2 * tm * d * 4
    return pl.pallas_call(
        functools.partial(_ffn_kernel, alpha),
        grid=(bsz, s_len // tm),
        in_specs=[row, full(w_gate), full(w_up), full(w_down), full(g), full(b)],
        out_specs=row,
        out_shape=jax.ShapeDtypeStruct((bsz, s_len, d), F32),
        compiler_params=pltpu.CompilerParams(
            dimension_semantics=("parallel", "parallel"),
            vmem_limit_bytes=_vmem_limit(est)),
        name="ffn_ln2",
    )(h, w_gate, w_up, w_down, g, b)


def kernel(x, w_in, ret_gn_gain, w_out, ln1_g, ln1_b, w_gate, w_up, w_down, ln2_g, ln2_b):
    bsz, s_len, d = x.shape
    depth = w_in.shape[0]
    assert s_len % math.lcm(RET_CHUNK, MOBA_BLOCK, TOKEN_TILE) == 0
    assert w_in.shape[2] == 4 * RET_WIDTH + 3 * MOBA_WIDTH
    alpha = (2 * depth) ** 0.25
    ret_tables = _retention_tables()
    alibi_tab, slopes = _alibi_tables()
    r4 = 4 * RET_WIDTH
    h = x
    for layer in range(depth):
        w = w_in[layer]
        w_nat = jnp.concatenate([w[:, :r4], w[:, r4 + MOBA_WIDTH:r4 + 2 * MOBA_WIDTH]], axis=1).astype(BF16)
        w_tr = jnp.concatenate([w[:, r4:r4 + MOBA_WIDTH], w[:, r4 + 2 * MOBA_WIDTH:]], axis=1).T.astype(BF16)
        rqkv, rg, mk, mqt, mvt = _inproj(h, w_nat, w_tr)
        ret = _retention(rqkv, rg, ret_gn_gain[layer][None, :], ret_tables)
        moba = _moba(mqt, mk, mvt, alibi_tab, slopes)
        wo = w_out[layer].astype(BF16)
        h = _outproj(h, ret, moba, wo[:RET_WIDTH], wo[RET_WIDTH:],
                     ln1_g[layer][None, :], ln1_b[layer][None, :], alpha)
        h = _ffn(h, w_gate[layer].astype(BF16), w_up[layer].astype(BF16), w_down[layer].astype(BF16),
                 ln2_g[layer][None, :], ln2_b[layer][None, :], alpha)
    return h
```

```python
import functools
import math

import numpy as np
import jax
import jax.numpy as jnp
from jax import lax
from jax.experimental import pallas as pl
from jax.experimental.pallas import tpu as pltpu

F32 = jnp.float32
BF16 = jnp.bfloat16

RET_HEADS = 4
RET_HEAD_DIM = 128
RET_WIDTH = RET_HEADS * RET_HEAD_DIM
RET_CHUNK = 256
MOBA_HEADS = 8
MOBA_HEAD_DIM = 64
MOBA_WIDTH = MOBA_HEADS * MOBA_HEAD_DIM
MOBA_BLOCK = 256
MOBA_TOP_K = 3
MOBA_GROUP = 2
NORM_EPS = 1e-5
NEG_INF = -1e30

V7X_VMEM_BYTES = 64 * 1024 * 1024
LANES = 128
TOKEN_TILE = 512


def _vmem_limit(estimate_bytes):
    return int(min(V7X_VMEM_BYTES - 4 * 1024 * 1024, max(estimate_bytes, 16 * 1024 * 1024)))


def _nt_dot(a, b):
    return lax.dot_general(a, b, (((1,), (1,)), ((), ())), preferred_element_type=F32)


def _layer_norm(v, g, b):
    mu = jnp.mean(v, axis=-1, keepdims=True)
    d = v - mu
    var = jnp.mean(d * d, axis=-1, keepdims=True)
    return d * lax.rsqrt(var + NORM_EPS) * g + b


def _inproj_kernel(x_ref, wn_ref, wt_ref, rqkv_ref, rg_ref, mk_ref, mqt_ref, mvt_ref):
    x = x_ref[0].astype(BF16)
    nat = jnp.dot(x, wn_ref[...], preferred_element_type=F32)
    rqkv_ref[0] = nat[:, :3 * RET_WIDTH].astype(BF16)
    rg_ref[0] = nat[:, 3 * RET_WIDTH:4 * RET_WIDTH]
    mk_ref[0] = nat[:, 4 * RET_WIDTH:].astype(BF16)
    tr = _nt_dot(wt_ref[...], x)
    for c in range(x.shape[0] // MOBA_BLOCK):
        cols = slice(c * MOBA_BLOCK, (c + 1) * MOBA_BLOCK)
        mqt_ref[0, c] = (tr[:MOBA_WIDTH, cols] * MOBA_HEAD_DIM ** -0.5).astype(BF16)
        mvt_ref[0, c] = tr[MOBA_WIDTH:, cols].astype(BF16)


def _inproj(x, w_nat, w_tr):
    bsz, s_len, d = x.shape
    tm = TOKEN_TILE
    nb = s_len // MOBA_BLOCK
    n_nat = w_nat.shape[1]
    est = (2 * tm * d * 4 + 2 * (w_nat.size + w_tr.size) * 2
           + 2 * tm * (3 * RET_WIDTH * 2 + RET_WIDTH * 4 + 3 * MOBA_WIDTH * 2)
           + tm * (n_nat + 2 * MOBA_WIDTH) * 4 * 2)
    return pl.pallas_call(
        _inproj_kernel,
        grid=(bsz, s_len // tm),
        in_specs=[
            pl.BlockSpec((1, tm, d), lambda b, t: (b, t, 0)),
            pl.BlockSpec((d, n_nat), lambda b, t: (0, 0)),
            pl.BlockSpec((2 * MOBA_WIDTH, d), lambda b, t: (0, 0)),
        ],
        out_specs=[
            pl.BlockSpec((1, tm, 3 * RET_WIDTH), lambda b, t: (b, t, 0)),
            pl.BlockSpec((1, tm, RET_WIDTH), lambda b, t: (b, t, 0)),
            pl.BlockSpec((1, tm, MOBA_WIDTH), lambda b, t: (b, t, 0)),
            pl.BlockSpec((1, tm // MOBA_BLOCK, MOBA_WIDTH, MOBA_BLOCK), lambda b, t: (b, t, 0, 0)),
            pl.BlockSpec((1, tm // MOBA_BLOCK, MOBA_WIDTH, MOBA_BLOCK), lambda b, t: (b, t, 0, 0)),
        ],
        out_shape=[
            jax.ShapeDtypeStruct((bsz, s_len, 3 * RET_WIDTH), BF16),
            jax.ShapeDtypeStruct((bsz, s_len, RET_WIDTH), F32),
            jax.ShapeDtypeStruct((bsz, s_len, MOBA_WIDTH), BF16),
            jax.ShapeDtypeStruct((bsz, nb, MOBA_WIDTH, MOBA_BLOCK), BF16),
            jax.ShapeDtypeStruct((bsz, nb, MOBA_WIDTH, MOBA_BLOCK), BF16),
        ],
        compiler_params=pltpu.CompilerParams(
            dimension_semantics=("parallel", "parallel"),
            vmem_limit_bytes=_vmem_limit(est)),
        name="inproj",
    )(x, w_nat, w_tr)


def _retention_tables():
    h = np.arange(RET_HEADS, dtype=np.float64)
    log_g = np.log1p(-np.exp2(-5.0 - h))
    pos = np.arange(RET_CHUNK, dtype=np.float64)
    diff = pos[:, None] - pos[None, :]
    scale = RET_HEAD_DIM ** -0.5
    intra = np.where(diff >= 0.0, np.exp(log_g[:, None, None] * np.maximum(diff, 0.0)), 0.0) * scale
    k_to_end = np.exp(log_g[:, None] * (RET_CHUNK - 1.0 - pos)[None, :]) * scale
    q_from_start = np.exp(log_g[:, None] * (pos + 1.0)[None, :])
    chunk_decay = np.exp(log_g * RET_CHUNK)
    bcast = lambda t: np.broadcast_to(t[:, :, None], (RET_HEADS, RET_CHUNK, RET_HEAD_DIM))
    return (jnp.asarray(intra, F32), jnp.asarray(bcast(k_to_end), F32),
            jnp.asarray(bcast(q_from_start), F32), jnp.asarray(chunk_decay, F32))


def _retention_kernel(cd_ref, q_ref, k_ref, v_ref, rg_ref, gain_ref, dec_ref, kte_ref, qfs_ref,
                      o_ref, state_ref):
    h = pl.program_id(1)

    @pl.when(pl.program_id(2) == 0)
    def _():
        state_ref[...] = jnp.zeros_like(state_ref)

    q = q_ref[0]
    k = k_ref[0]
    v = v_ref[0]
    scores = _nt_dot(q, k) * dec_ref[0]
    intra = jnp.dot(scores.astype(BF16), v, preferred_element_type=F32)
    prev = state_ref[...]
    cross = jnp.dot(q, prev.astype(BF16), preferred_element_type=F32) * qfs_ref[0]
    y = intra + cross
    k_dec = (k.astype(F32) * kte_ref[0]).T.astype(BF16)
    kv = jnp.dot(k_dec, v, preferred_element_type=F32)
    state_ref[...] = prev * cd_ref[h] + kv
    mu = jnp.mean(y, axis=-1, keepdims=True)
    d = y - mu
    var = jnp.mean(d * d, axis=-1, keepdims=True)
    yn = d * lax.rsqrt(var + NORM_EPS) * gain_ref[...]
    g = rg_ref[0]
    o_ref[0] = (g * jax.nn.sigmoid(g) * yn).astype(o_ref.dtype)


def _retention(rqkv, rg, gain, tables):
    bsz, s_len, _ = rqkv.shape
    dec, kte, qfs, cd = tables
    c, e = RET_CHUNK, RET_HEAD_DIM
    tab_spec = lambda shape: pl.BlockSpec((1,) + shape, lambda b, h, n: (h, 0, 0))
    return pl.pallas_call(
        _retention_kernel,
        grid=(bsz, RET_HEADS, s_len // c),
        in_specs=[
            pl.BlockSpec(memory_space=pltpu.SMEM),
            pl.BlockSpec((1, c, e), lambda b, h, n: (b, n, h)),
            pl.BlockSpec((1, c, e), lambda b, h, n: (b, n, RET_HEADS + h)),
            pl.BlockSpec((1, c, e), lambda b, h, n: (b, n, 2 * RET_HEADS + h)),
            pl.BlockSpec((1, c, e), lambda b, h, n: (b, n, h)),
            pl.BlockSpec((1, e), lambda b, h, n: (0, h)),
            tab_spec((c, c)), tab_spec((c, e)), tab_spec((c, e)),
        ],
        out_specs=pl.BlockSpec((1, c, e), lambda b, h, n: (b, n, h)),
        out_shape=jax.ShapeDtypeStruct((bsz, s_len, RET_WIDTH), BF16),
        scratch_shapes=[pltpu.VMEM((e, e), F32)],
        compiler_params=pltpu.CompilerParams(
            dimension_semantics=("parallel", "parallel", "arbitrary")),
        name="retention",
    )(cd, rqkv, rqkv, rqkv, rg, gain, dec, kte, qfs)


def _alibi_tables():
    slopes = np.exp2(-8.0 * (np.arange(MOBA_HEADS, dtype=np.float64) + 1.0) / MOBA_HEADS)
    r = np.arange(MOBA_BLOCK, dtype=np.float64)
    tab = -slopes[:, None, None] * (r[None, None, :] - r[None, :, None])
    return jnp.asarray(tab, F32), jnp.asarray(slopes, F32)


def _moba_kernel(slope_ref, qt_ref, k_ref, vt_ref, tab_ref, o_ref, kmean_ref, bias_ref,
                 sa_ref, sb_ref):
    blk, hd = MOBA_BLOCK, MOBA_HEAD_DIM
    n_b = vt_ref.shape[1]
    pair = pl.program_id(1)
    i = pl.program_id(2)

    @pl.when(i == 0)
    def _():
        for b in range(n_b):
            kb = k_ref[0, b * blk:(b + 1) * blk, :].astype(F32)
            kmean_ref[b:b + 1, :] = jnp.sum(kb, axis=0, keepdims=True) * (1.0 / blk)

    grp = MOBA_GROUP
    qt = qt_ref[0, 0]
    row = lax.broadcasted_iota(jnp.int32, qt.shape, 0)
    kmean = kmean_ref[...].astype(BF16)
    jidx = lax.broadcasted_iota(jnp.int32, (n_b, blk), 0)
    k_pos = lax.broadcasted_iota(jnp.int32, (blk, blk), 0)
    q_pos = lax.broadcasted_iota(jnp.int32, (blk, blk), 1)
    past = jidx < i

    def raw_scores(hl, t):
        start = t * (grp * blk)
        if not isinstance(t, int):
            start = pl.multiple_of(start, grp * blk)
        return jnp.dot(k_ref[0, pl.ds(start, grp * blk), :], qhs[hl], preferred_element_type=F32)

    def tile_values(hl, t):
        return jnp.concatenate(
            [vt_ref[0, t * grp + c, hl * hd:(hl + 1) * hd, :] for c in range(grp)], axis=1)

    def col_max(u3):
        return jnp.max(jnp.max(u3, axis=0), axis=0, keepdims=True)

    def col_sum(p3):
        return jnp.sum(jnp.sum(p3, axis=0), axis=0, keepdims=True)

    def consume(hl, s_ref, t, m, l, acc):
        u = s_ref[hl].reshape(grp, blk, blk) + tab_ref[hl][None] + bias_ref[hl, t][:, None, :]
        m_new = jnp.maximum(m, col_max(u))
        alpha = jnp.exp(m - m_new)
        p = jnp.exp(u - m_new[None])
        l = alpha * l + col_sum(p)
        acc = alpha * acc + jnp.dot(tile_values(hl, t), p.reshape(grp * blk, blk).astype(BF16),
                                    preferred_element_type=F32)
        return m_new, l, acc

    qhs, state = [], []
    for hl in range(2):
        slope = slope_ref[2 * pair + hl]
        qh = jnp.where((row >= hl * hd) & (row < (hl + 1) * hd), qt, jnp.zeros_like(qt))
        qhs.append(qh)
        gate = jnp.dot(kmean, qh, preferred_element_type=F32)
        taken = jnp.zeros((n_b, blk), jnp.bool_)
        for _ in range(MOBA_TOP_K):
            cand = past & jnp.logical_not(taken)
            best = jnp.max(jnp.where(cand, gate, -jnp.inf), axis=0, keepdims=True)
            pick = cand & (gate == best)
            first = jnp.min(jnp.where(pick, jidx, n_b), axis=0, keepdims=True)
            taken = taken | (jidx == first)
        block_off = (i - jidx).astype(F32) * (-float(blk) * slope)
        bias = jnp.where(taken, block_off, NEG_INF)
        for g in range(n_b // grp):
            bias_ref[hl, g] = bias[g * grp:(g + 1) * grp, :]

        kd = k_ref[0, pl.ds(pl.multiple_of(i * blk, blk), blk), :]
        ud = jnp.where(k_pos <= q_pos,
                       jnp.dot(kd, qh, preferred_element_type=F32) + tab_ref[hl], NEG_INF)
        m = jnp.max(ud, axis=0, keepdims=True)
        pd = jnp.exp(ud - m)
        l = jnp.sum(pd, axis=0, keepdims=True)
        acc = jnp.dot(vt_ref[0, i, hl * hd:(hl + 1) * hd, :], pd.astype(BF16),
                      preferred_element_type=F32)
        state += [m, l, acc]
        sa_ref[hl] = raw_scores(hl, 0)

    last_tile = n_b // grp - 1

    def body(tt, carry):
        carry = list(carry)
        t0 = 2 * tt
        for hl in range(2):
            sb_ref[hl] = raw_scores(hl, t0 + 1)
        for hl in range(2):
            carry[3 * hl:3 * hl + 3] = consume(hl, sa_ref, t0, *carry[3 * hl:3 * hl + 3])
        for hl in range(2):
            sa_ref[hl] = raw_scores(hl, jnp.minimum(t0 + 2, last_tile))
        for hl in range(2):
            carry[3 * hl:3 * hl + 3] = consume(hl, sb_ref, t0 + 1, *carry[3 * hl:3 * hl + 3])
        return tuple(carry)

    n_tiles = (i + grp - 1) // grp
    state = lax.fori_loop(0, (n_tiles + 1) // 2, body, tuple(state))
    outs = [state[3 * hl + 2] * (1.0 / state[3 * hl + 1]) for hl in range(2)]
    o_ref[0] = jnp.concatenate(outs, axis=0).T.astype(o_ref.dtype)


def _moba(mqt, mk, mvt, tab, slopes):
    bsz, n_b, width, blk = mqt.shape
    s_len = mk.shape[1]
    pw = 2 * MOBA_HEAD_DIM
    est = 2 * (2 * s_len * pw * 2) + 4 * 2 * blk * blk * 4 + 2 * 4 * (MOBA_GROUP + 1) * blk * blk * 4
    return pl.pallas_call(
        _moba_kernel,
        grid=(bsz, width // pw, n_b),
        in_specs=[
            pl.BlockSpec(memory_space=pltpu.SMEM),
            pl.BlockSpec((1, 1, pw, blk), lambda b, p, i: (b, i, p, 0)),
            pl.BlockSpec((1, s_len, pw), lambda b, p, i: (b, 0, p)),
            pl.BlockSpec((1, n_b, pw, blk), lambda b, p, i: (b, 0, p, 0)),
            pl.BlockSpec((2, blk, blk), lambda b, p, i: (p, 0, 0)),
        ],
        out_specs=pl.BlockSpec((1, blk, pw), lambda b, p, i: (b, i, p)),
        out_shape=jax.ShapeDtypeStruct((bsz, s_len, width), BF16),
        scratch_shapes=[pltpu.VMEM((n_b, pw), F32),
                        pltpu.VMEM((2, n_b // MOBA_GROUP, MOBA_GROUP, blk), F32),
                        pltpu.VMEM((2, MOBA_GROUP * blk, blk), F32),
                        pltpu.VMEM((2, MOBA_GROUP * blk, blk), F32)],
        compiler_params=pltpu.CompilerParams(
            dimension_semantics=("parallel", "parallel", "arbitrary"),
            vmem_limit_bytes=_vmem_limit(est)),
        name="moba",
    )(slopes, mqt, mk, mvt, tab)


def _outproj_kernel(alpha, x_ref, ret_ref, moba_ref, wr_ref, wm_ref, g_ref, b_ref, o_ref):
    mixed = (jnp.dot(ret_ref[0], wr_ref[...], preferred_element_type=F32)
             + jnp.dot(moba_ref[0], wm_ref[...], preferred_element_type=F32))
    o_ref[0] = _layer_norm(alpha * x_ref[0] + mixed, g_ref[...], b_ref[...])


def _outproj(x, ret, moba, w_ret, w_moba, g, b, alpha):
    bsz, s_len, d = x.shape
    tm = TOKEN_TILE
    row = lambda w: pl.BlockSpec((1, tm, w), lambda b_, t: (b_, t, 0))
    full = lambda a: pl.BlockSpec(a.shape, lambda b_, t: (0, 0))
    est = 2 * (2 * tm * d * 4 + 2 * tm * RET_WIDTH * 2 + 2 * d * d * 2) + 4 * tm * d * 4
    return pl.pallas_call(
        functools.partial(_outproj_kernel, alpha),
        grid=(bsz, s_len // tm),
        in_specs=[row(d), row(RET_WIDTH), row(MOBA_WIDTH), full(w_ret), full(w_moba), full(g), full(b)],
        out_specs=row(d),
        out_shape=jax.ShapeDtypeStruct((bsz, s_len, d), F32),
        compiler_params=pltpu.CompilerParams(
            dimension_semantics=("parallel", "parallel"),
            vmem_limit_bytes=_vmem_limit(est)),
        name="outproj_ln1",
    )(x, ret, moba, w_ret, w_moba, g, b)


def _ffn_kernel(alpha, h_ref, wg_ref, wu_ref, wd_ref, g_ref, b_ref, o_ref):
    h = h_ref[0]
    hb = h.astype(BF16)
    gate = jnp.dot(hb, wg_ref[...], preferred_element_type=F32)
    up = jnp.dot(hb, wu_ref[...], preferred_element_type=F32)
    act = (gate * jax.nn.sigmoid(gate) * up).astype(BF16)
    ffn = jnp.dot(act, wd_ref[...], preferred_element_type=F32)
    o_ref[0] = _layer_norm(alpha * h + ffn, g_ref[...], b_ref[...])


def _ffn(h, w_gate, w_up, w_down, g, b, alpha):
    bsz, s_len, d = h.shape
    d_ff = w_gate.shape[1]
    tm = TOKEN_TILE
    row = pl.BlockSpec((1, tm, d), lambda b_, t: (b_, t, 0))
    full = lambda a: pl.BlockSpec(a.shape, lambda b_, t: (0, 0), pipeline_mode=pl.Buffered(1))
    est = 3 * d * d_ff * 2 + 4 * tm * d * 4 + tm * d_ff * (4 + 4 + 2) + 2 * tm * d * 4
    return pl.pallas_call(
        functools.partial(_ffn_kernel, alpha),
        grid=(bsz, s_len // tm),
        in_specs=[row, full(w_gate), full(w_up), full(w_down), full(g), full(b)],
        out_specs=row,
        out_shape=jax.ShapeDtypeStruct((bsz, s_len, d), F32),
        compiler_params=pltpu.CompilerParams(
            dimension_semantics=("parallel", "parallel"),
            vmem_limit_bytes=_vmem_limit(est)),
        name="ffn_ln2",
    )(h, w_gate, w_up, w_down, g, b)


def kernel(x, w_in, ret_gn_gain, w_out, ln1_g, ln1_b, w_gate, w_up, w_down, ln2_g, ln2_b):
    bsz, s_len, d = x.shape
    depth = w_in.shape[0]
    assert s_len % math.lcm(RET_CHUNK, MOBA_BLOCK, TOKEN_TILE) == 0
    assert w_in.shape[2] == 4 * RET_WIDTH + 3 * MOBA_WIDTH
    alpha = (2 * depth) ** 0.25
    ret_tables = _retention_tables()
    alibi_tab, slopes = _alibi_tables()
    r4 = 4 * RET_WIDTH
    h = x
    for layer in range(depth):
        w = w_in[layer]
        w_nat = jnp.concatenate([w[:, :r4], w[:, r4 + MOBA_WIDTH:r4 + 2 * MOBA_WIDTH]], axis=1).astype(BF16)
        w_tr = jnp.concatenate([w[:, r4:r4 + MOBA_WIDTH], w[:, r4 + 2 * MOBA_WIDTH:]], axis=1).T.astype(BF16)
        rqkv, rg, mk, mqt, mvt = _inproj(h, w_nat, w_tr)
        ret = _retention(rqkv, rg, ret_gn_gain[layer][None, :], ret_tables)
        moba = _moba(mqt, mk, mvt, alibi_tab, slopes)
        wo = w_out[layer].astype(BF16)
        h = _outproj(h, ret, moba, wo[:RET_WIDTH], wo[RET_WIDTH:],
                     ln1_g[layer][None, :], ln1_b[layer][None, :], alpha)
        h = _ffn(h, w_gate[layer].astype(BF16), w_up[layer].astype(BF16), w_down[layer].astype(BF16),
                 ln2_g[layer][None, :], ln2_b[layer][None, :], alpha)
    return h
```

```python
import functools
import math

import numpy as np
import jax
import jax.numpy as jnp
from jax import lax
from jax.experimental import pallas as pl
from jax.experimental.pallas import tpu as pltpu

F32 = jnp.float32
BF16 = jnp.bfloat16

RET_HEADS = 4
RET_HEAD_DIM = 128
RET_WIDTH = RET_HEADS * RET_HEAD_DIM
RET_CHUNK = 256
MOBA_HEADS = 8
MOBA_HEAD_DIM = 64
MOBA_WIDTH = MOBA_HEADS * MOBA_HEAD_DIM
MOBA_BLOCK = 256
MOBA_TOP_K = 3
MOBA_GROUP = 2
MOBA_KEY_LANES = 128
MOBA_AUX_ROWS = 16
NORM_EPS = 1e-5
NEG_INF = -1e30

V7X_VMEM_BYTES = 64 * 1024 * 1024
LANES = 128
TOKEN_TILE = 512


def _vmem_limit(estimate_bytes):
    return int(min(V7X_VMEM_BYTES - 4 * 1024 * 1024, max(estimate_bytes, 16 * 1024 * 1024)))


def _nt_dot(a, b):
    return lax.dot_general(a, b, (((1,), (1,)), ((), ())), preferred_element_type=F32)


def _layer_norm(v, g, b):
    mu = jnp.mean(v, axis=-1, keepdims=True)
    d = v - mu
    var = jnp.mean(d * d, axis=-1, keepdims=True)
    return d * lax.rsqrt(var + NORM_EPS) * g + b


def _key_side_pattern(tm):
    assert (tm // MOBA_BLOCK) % MOBA_GROUP == 0 and 2 + MOBA_GROUP <= MOBA_AUX_ROWS
    pos = np.arange(tm)
    pat = np.zeros((tm, MOBA_HEADS, MOBA_KEY_LANES), np.float32)
    pat[:, :, MOBA_HEAD_DIM] = (pos % MOBA_BLOCK)[:, None]
    pat[:, :, MOBA_HEAD_DIM + 1] = 1.0
    for c in range(MOBA_GROUP):
        pat[:, :, MOBA_HEAD_DIM + 2 + c] = ((pos // MOBA_BLOCK) % MOBA_GROUP == c)[:, None]
    return jnp.asarray(pat.reshape(tm, MOBA_HEADS * MOBA_KEY_LANES))


def _inproj_kernel(x_ref, wn_ref, wt_ref, kpat_ref, rqkv_ref, rg_ref, mk_ref, mqt_ref, mvt_ref):
    x = x_ref[0].astype(BF16)
    nat = jnp.dot(x, wn_ref[...], preferred_element_type=F32)
    rqkv_ref[0] = nat[:, :3 * RET_WIDTH].astype(BF16)
    rg_ref[0] = nat[:, 3 * RET_WIDTH:4 * RET_WIDTH]
    mk_ref[0] = (nat[:, 4 * RET_WIDTH:] + kpat_ref[...]).astype(BF16)
    tr = _nt_dot(wt_ref[...], x)
    for c in range(x.shape[0] // MOBA_BLOCK):
        cols = slice(c * MOBA_BLOCK, (c + 1) * MOBA_BLOCK)
        mqt_ref[0, c] = (tr[:MOBA_WIDTH, cols] * MOBA_HEAD_DIM ** -0.5).astype(BF16)
        mvt_ref[0, c] = tr[MOBA_WIDTH:, cols].astype(BF16)


def _inproj(x, w_nat, w_tr):
    bsz, s_len, d = x.shape
    tm = TOKEN_TILE
    nb = s_len // MOBA_BLOCK
    n_nat = w_nat.shape[1]
    k_width = MOBA_HEADS * MOBA_KEY_LANES
    kpat = _key_side_pattern(tm)
    est = (2 * tm * d * 4 + 2 * (w_nat.size + w_tr.size) * 2 + 2 * kpat.size * 4
           + 2 * tm * (3 * RET_WIDTH * 2 + RET_WIDTH * 4 + (k_width + 2 * MOBA_WIDTH) * 2)
           + tm * (n_nat + 2 * MOBA_WIDTH) * 4 * 2)
    return pl.pallas_call(
        _inproj_kernel,
        grid=(bsz, s_len // tm),
        in_specs=[
            pl.BlockSpec((1, tm, d), lambda b, t: (b, t, 0)),
            pl.BlockSpec((d, n_nat), lambda b, t: (0, 0)),
            pl.BlockSpec((2 * MOBA_WIDTH, d), lambda b, t: (0, 0)),
            pl.BlockSpec((tm, k_width), lambda b, t: (0, 0)),
        ],
        out_specs=[
            pl.BlockSpec((1, tm, 3 * RET_WIDTH), lambda b, t: (b, t, 0)),
            pl.BlockSpec((1, tm, RET_WIDTH), lambda b, t: (b, t, 0)),
            pl.BlockSpec((1, tm, k_width), lambda b, t: (b, t, 0)),
            pl.BlockSpec((1, tm // MOBA_BLOCK, MOBA_WIDTH, MOBA_BLOCK), lambda b, t: (b, t, 0, 0)),
            pl.BlockSpec((1, tm // MOBA_BLOCK, MOBA_WIDTH, MOBA_BLOCK), lambda b, t: (b, t, 0, 0)),
        ],
        out_shape=[
            jax.ShapeDtypeStruct((bsz, s_len, 3 * RET_WIDTH), BF16),
            jax.ShapeDtypeStruct((bsz, s_len, RET_WIDTH), F32),
            jax.ShapeDtypeStruct((bsz, s_len, k_width), BF16),
            jax.ShapeDtypeStruct((bsz, nb, MOBA_WIDTH, MOBA_BLOCK), BF16),
            jax.ShapeDtypeStruct((bsz, nb, MOBA_WIDTH, MOBA_BLOCK), BF16),
        ],
        compiler_params=pltpu.CompilerParams(
            dimension_semantics=("parallel", "parallel"),
            vmem_limit_bytes=_vmem_limit(est)),
        name="inproj",
    )(x, w_nat, w_tr, kpat)


def _retention_tables():
    h = np.arange(RET_HEADS, dtype=np.float64)
    log_g = np.log1p(-np.exp2(-5.0 - h))
    pos = np.arange(RET_CHUNK, dtype=np.float64)
    diff = pos[:, None] - pos[None, :]
    scale = RET_HEAD_DIM ** -0.5
    intra = np.where(diff >= 0.0, np.exp(log_g[:, None, None] * np.maximum(diff, 0.0)), 0.0) * scale
    k_to_end = np.exp(log_g[:, None] * (RET_CHUNK - 1.0 - pos)[None, :]) * scale
    q_from_start = np.exp(log_g[:, None] * (pos + 1.0)[None, :])
    chunk_decay = np.exp(log_g * RET_CHUNK)
    bcast = lambda t: np.broadcast_to(t[:, :, None], (RET_HEADS, RET_CHUNK, RET_HEAD_DIM))
    return (jnp.asarray(intra, F32), jnp.asarray(bcast(k_to_end), F32),
            jnp.asarray(bcast(q_from_start), F32), jnp.asarray(chunk_decay, F32))


def _retention_kernel(cd_ref, q_ref, k_ref, v_ref, rg_ref, gain_ref, dec_ref, kte_ref, qfs_ref,
                      o_ref, state_ref):
    h = pl.program_id(1)

    @pl.when(pl.program_id(2) == 0)
    def _():
        state_ref[...] = jnp.zeros_like(state_ref)

    q = q_ref[0]
    k = k_ref[0]
    v = v_ref[0]
    scores = _nt_dot(q, k) * dec_ref[0]
    intra = jnp.dot(scores.astype(BF16), v, preferred_element_type=F32)
    prev = state_ref[...]
    cross = jnp.dot(q, prev.astype(BF16), preferred_element_type=F32) * qfs_ref[0]
    y = intra + cross
    k_dec = (k.astype(F32) * kte_ref[0]).T.astype(BF16)
    kv = jnp.dot(k_dec, v, preferred_element_type=F32)
    state_ref[...] = prev * cd_ref[h] + kv
    mu = jnp.mean(y, axis=-1, keepdims=True)
    d = y - mu
    var = jnp.mean(d * d, axis=-1, keepdims=True)
    yn = d * lax.rsqrt(var + NORM_EPS) * gain_ref[...]
    g = rg_ref[0]
    o_ref[0] = (g * jax.nn.sigmoid(g) * yn).astype(o_ref.dtype)


def _retention(rqkv, rg, gain, tables):
    bsz, s_len, _ = rqkv.shape
    dec, kte, qfs, cd = tables
    c, e = RET_CHUNK, RET_HEAD_DIM
    tab_spec = lambda shape: pl.BlockSpec((1,) + shape, lambda b, h, n: (h, 0, 0))
    return pl.pallas_call(
        _retention_kernel,
        grid=(bsz, RET_HEADS, s_len // c),
        in_specs=[
            pl.BlockSpec(memory_space=pltpu.SMEM),
            pl.BlockSpec((1, c, e), lambda b, h, n: (b, n, h)),
            pl.BlockSpec((1, c, e), lambda b, h, n: (b, n, RET_HEADS + h)),
            pl.BlockSpec((1, c, e), lambda b, h, n: (b, n, 2 * RET_HEADS + h)),
            pl.BlockSpec((1, c, e), lambda b, h, n: (b, n, h)),
            pl.BlockSpec((1, e), lambda b, h, n: (0, h)),
            tab_spec((c, c)), tab_spec((c, e)), tab_spec((c, e)),
        ],
        out_specs=pl.BlockSpec((1, c, e), lambda b, h, n: (b, n, h)),
        out_shape=jax.ShapeDtypeStruct((bsz, s_len, RET_WIDTH), BF16),
        scratch_shapes=[pltpu.VMEM((e, e), F32)],
        compiler_params=pltpu.CompilerParams(
            dimension_semantics=("parallel", "parallel", "arbitrary")),
        name="retention",
    )(cd, rqkv, rqkv, rqkv, rg, gain, dec, kte, qfs)


def _alibi_slopes():
    slopes = np.exp2(-8.0 * (np.arange(MOBA_HEADS, dtype=np.float64) + 1.0) / MOBA_HEADS)
    return jnp.asarray(slopes, F32)


def _moba_kernel(slope_ref, qt_ref, k_ref, vt_ref, o_ref, kmean_ref, aux_ref, sa_ref, sb_ref):
    blk, hd, grp = MOBA_BLOCK, MOBA_HEAD_DIM, MOBA_GROUP
    kl, aux_rows = MOBA_KEY_LANES, MOBA_AUX_ROWS
    n_b = vt_ref.shape[1]
    pair = pl.program_id(1)
    i = pl.program_id(2)

    @pl.when(i == 0)
    def _():
        for b in range(n_b):
            kb = k_ref[0, b * blk:(b + 1) * blk, :].astype(F32)
            kmean_ref[b:b + 1, :] = jnp.sum(kb, axis=0, keepdims=True) * (1.0 / blk)

    jidx = lax.broadcasted_iota(jnp.int32, (n_b, blk), 0)
    k_pos = lax.broadcasted_iota(jnp.int32, (blk, blk), 0)
    q_pos = lax.broadcasted_iota(jnp.int32, (blk, blk), 1)
    aux_row = lax.broadcasted_iota(jnp.int32, (aux_rows, blk), 0)
    r_q = lax.broadcasted_iota(jnp.int32, (aux_rows, blk), 1).astype(F32)
    past = jidx < i
    q_tops = [qt_ref[0, 0, hl * hd:(hl + 1) * hd, :] for hl in range(2)]
    zero_rows = lambda n: jnp.zeros((n, blk), BF16)

    def q_aug(hl, aux):
        return jnp.concatenate([q_tops[hl], aux.astype(BF16), zero_rows(kl - hd - aux_rows)], axis=0)

    def raw_scores(hl, t):
        start = t * (grp * blk)
        if not isinstance(t, int):
            start = pl.multiple_of(start, grp * blk)
        k_rows = k_ref[0, pl.ds(start, grp * blk), hl * kl:(hl + 1) * kl]
        return jnp.dot(k_rows, q_aug(hl, aux_ref[hl, t]), preferred_element_type=F32)

    def tile_values(hl, t):
        return jnp.concatenate(
            [vt_ref[0, t * grp + c, hl * hd:(hl + 1) * hd, :] for c in range(grp)], axis=1)

    def col_max(u3):
        return jnp.max(jnp.max(u3, axis=0), axis=0, keepdims=True)

    def col_sum(p3):
        return jnp.sum(jnp.sum(p3, axis=0), axis=0, keepdims=True)

    def consume(hl, s_ref, t, m, l, acc):
        u = s_ref[hl].reshape(grp, blk, blk)
        m_new = jnp.maximum(m, col_max(u))
        alpha = jnp.exp(m - m_new)
        p = jnp.exp(u - m_new[None])
        l = alpha * l + col_sum(p)
        acc = alpha * acc + jnp.dot(tile_values(hl, t), p.reshape(grp * blk, blk).astype(BF16),
                                    preferred_element_type=F32)
        return m_new, l, acc

    state = []
    for hl in range(2):
        slope = slope_ref[2 * pair + hl]
        kmean = kmean_ref[:, hl * kl:(hl + 1) * kl].astype(BF16)
        q_gate = jnp.concatenate([q_tops[hl], zero_rows(kl - hd)], axis=0)
        gate = jnp.dot(kmean, q_gate, preferred_element_type=F32)
        taken = jnp.zeros((n_b, blk), jnp.bool_)
        for _ in range(MOBA_TOP_K):
            cand = past & jnp.logical_not(taken)
            best = jnp.max(jnp.where(cand, gate, -jnp.inf), axis=0, keepdims=True)
            pick = cand & (gate == best)
            first = jnp.min(jnp.where(pick, jidx, n_b), axis=0, keepdims=True)
            taken = taken | (jidx == first)
        block_off = (i - jidx).astype(F32) * (-float(blk) * slope)
        bias = jnp.where(taken, block_off, NEG_INF)
        alibi_rows = jnp.where(aux_row == 0, slope, jnp.where(aux_row == 1, -slope * r_q, 0.0))
        for t in range(n_b // grp):
            aux = alibi_rows
            for c in range(grp):
                aux = jnp.where(aux_row == 2 + c, bias[t * grp + c:t * grp + c + 1, :], aux)
            aux_ref[hl, t] = aux

        kd = k_ref[0, pl.ds(pl.multiple_of(i * blk, blk), blk), hl * kl:(hl + 1) * kl]
        ud = jnp.where(k_pos <= q_pos,
                       jnp.dot(kd, q_aug(hl, alibi_rows), preferred_element_type=F32), NEG_INF)
        m = jnp.max(ud, axis=0, keepdims=True)
        pd = jnp.exp(ud - m)
        l = jnp.sum(pd, axis=0, keepdims=True)
        acc = jnp.dot(vt_ref[0, i, hl * hd:(hl + 1) * hd, :], pd.astype(BF16),
                      preferred_element_type=F32)
        state += [m, l, acc]
        sa_ref[hl] = raw_scores(hl, 0)

    last_tile = n_b // grp - 1

    def body(tt, carry):
        carry = list(carry)
        t0 = 2 * tt
        for hl in range(2):
            sb_ref[hl] = raw_scores(hl, t0 + 1)
        for hl in range(2):
            carry[3 * hl:3 * hl + 3] = consume(hl, sa_ref, t0, *carry[3 * hl:3 * hl + 3])
        for hl in range(2):
            sa_ref[hl] = raw_scores(hl, jnp.minimum(t0 + 2, last_tile))
        for hl in range(2):
            carry[3 * hl:3 * hl + 3] = consume(hl, sb_ref, t0 + 1, *carry[3 * hl:3 * hl + 3])
        return tuple(carry)

    n_tiles = (i + grp - 1) // grp
    state = lax.fori_loop(0, (n_tiles + 1) // 2, body, tuple(state))
    outs = [state[3 * hl + 2] * (1.0 / state[3 * hl + 1]) for hl in range(2)]
    o_ref[0] = jnp.concatenate(outs, axis=0).T.astype(o_ref.dtype)


def _moba(mqt, mk, mvt, slopes):
    bsz, n_b, width, blk = mqt.shape
    s_len = mk.shape[1]
    pw = 2 * MOBA_HEAD_DIM
    kw = 2 * MOBA_KEY_LANES
    tile = MOBA_GROUP * blk
    est = 2 * s_len * (kw + pw) * 2 + 2 * 2 * tile * blk * 4 + 2 * 4 * tile * blk * 4
    return pl.pallas_call(
        _moba_kernel,
        grid=(bsz, width // pw, n_b),
        in_specs=[
            pl.BlockSpec(memory_space=pltpu.SMEM),
            pl.BlockSpec((1, 1, pw, blk), lambda b, p, i: (b, i, p, 0)),
            pl.BlockSpec((1, s_len, kw), lambda b, p, i: (b, 0, p)),
            pl.BlockSpec((1, n_b, pw, blk), lambda b, p, i: (b, 0, p, 0)),
        ],
        out_specs=pl.BlockSpec((1, blk, pw), lambda b, p, i: (b, i, p)),
        out_shape=jax.ShapeDtypeStruct((bsz, s_len, width), BF16),
        scratch_shapes=[pltpu.VMEM((n_b, kw), F32),
                        pltpu.VMEM((2, n_b // MOBA_GROUP, MOBA_AUX_ROWS, blk), F32),
                        pltpu.VMEM((2, tile, blk), F32),
                        pltpu.VMEM((2, tile, blk), F32)],
        compiler_params=pltpu.CompilerParams(
            dimension_semantics=("parallel", "parallel", "arbitrary"),
            vmem_limit_bytes=_vmem_limit(est)),
        name="moba",
    )(slopes, mqt, mk, mvt)


def _outproj_kernel(alpha, x_ref, ret_ref, moba_ref, wr_ref, wm_ref, g_ref, b_ref, o_ref):
    mixed = (jnp.dot(ret_ref[0], wr_ref[...], preferred_element_type=F32)
             + jnp.dot(moba_ref[0], wm_ref[...], preferred_element_type=F32))
    o_ref[0] = _layer_norm(alpha * x_ref[0] + mixed, g_ref[...], b_ref[...])


def _outproj(x, ret, moba, w_ret, w_moba, g, b, alpha):
    bsz, s_len, d = x.shape
    tm = TOKEN_TILE
    row = lambda w: pl.BlockSpec((1, tm, w), lambda b_, t: (b_, t, 0))
    full = lambda a: pl.BlockSpec(a.shape, lambda b_, t: (0, 0))
    est = 2 * (2 * tm * d * 4 + 2 * tm * RET_WIDTH * 2 + 2 * d * d * 2) + 4 * tm * d * 4
    return pl.pallas_call(
        functools.partial(_outproj_kernel, alpha),
        grid=(bsz, s_len // tm),
        in_specs=[row(d), row(RET_WIDTH), row(MOBA_WIDTH), full(w_ret), full(w_moba), full(g), full(b)],
        out_specs=row(d),
        out_shape=jax.ShapeDtypeStruct((bsz, s_len, d), F32),
        compiler_params=pltpu.CompilerParams(
            dimension_semantics=("parallel", "parallel"),
            vmem_limit_bytes=_vmem_limit(est)),
        name="outproj_ln1",
    )(x, ret, moba, w_ret, w_moba, g, b)


def _ffn_kernel(alpha, h_ref, wg_ref, wu_ref, wd_ref, g_ref, b_ref, o_ref):
    h = h_ref[0]
    hb = h.astype(BF16)
    gate = jnp.dot(hb, wg_ref[...], preferred_element_type=F32)
    up = jnp.dot(hb, wu_ref[...], preferred_element_type=F32)
    act = (gate * jax.nn.sigmoid(gate) * up).astype(BF16)
    ffn = jnp.dot(act, wd_ref[...], preferred_element_type=F32)
    o_ref[0] = _layer_norm(alpha * h + ffn, g_ref[...], b_ref[...])


def _ffn(h, w_gate, w_up, w_down, g, b, alpha):
    bsz, s_len, d = h.shape
    d_ff = w_gate.shape[1]
    tm = TOKEN_TILE
    row = pl.BlockSpec((1, tm, d), lambda b_, t: (b_, t, 0))
    full = lambda a: pl.BlockSpec(a.shape, lambda b_, t: (0, 0), pipeline_mode=pl.Buffered(1))
    est = 3 * d * d_ff * 2 + 4 * tm * d * 4 + tm * d_ff * (4 + 4 + 2) + 2 * tm * d * 4
    return pl.pallas_call(
        functools.partial(_ffn_kernel, alpha),
        grid=(bsz, s_len // tm),
        in_specs=[row, full(w_gate), full(w_up), full(w_down), full(g), full(b)],
        out_specs=row,
        out_shape=jax.ShapeDtypeStruct((bsz, s_len, d), F32),
        compiler_params=pltpu.CompilerParams(
            dimension_semantics=("parallel", "parallel"),
            vmem_limit_bytes=_vmem_limit(est)),
        name="ffn_ln2",
    )(h, w_gate, w_up, w_down, g, b)


def kernel(x, w_in, ret_gn_gain, w_out, ln1_g, ln1_b, w_gate, w_up, w_down, ln2_g, ln2_b):
    bsz, s_len, d = x.shape
    depth = w_in.shape[0]
    assert s_len % math.lcm(RET_CHUNK, MOBA_BLOCK, TOKEN_TILE) == 0
    assert w_in.shape[2] == 4 * RET_WIDTH + 3 * MOBA_WIDTH
    alpha = (2 * depth) ** 0.25
    ret_tables = _retention_tables()
    slopes = _alibi_slopes()
    r4 = 4 * RET_WIDTH
    h = x
    for layer in range(depth):
        w = w_in[layer]
        w_mk = w[:, r4 + MOBA_WIDTH:r4 + 2 * MOBA_WIDTH].reshape(d, MOBA_HEADS, MOBA_HEAD_DIM)
        w_mk = jnp.pad(w_mk, ((0, 0), (0, 0), (0, MOBA_KEY_LANES - MOBA_HEAD_DIM)))
        w_nat = jnp.concatenate([w[:, :r4], w_mk.reshape(d, MOBA_HEADS * MOBA_KEY_LANES)], axis=1).astype(BF16)
        w_tr = jnp.concatenate([w[:, r4:r4 + MOBA_WIDTH], w[:, r4 + 2 * MOBA_WIDTH:]], axis=1).T.astype(BF16)
        rqkv, rg, mk, mqt, mvt = _inproj(h, w_nat, w_tr)
        ret = _retention(rqkv, rg, ret_gn_gain[layer][None, :], ret_tables)
        moba = _moba(mqt, mk, mvt, slopes)
        wo = w_out[layer].astype(BF16)
        h = _outproj(h, ret, moba, wo[:RET_WIDTH], wo[RET_WIDTH:],
                     ln1_g[layer][None, :], ln1_b[layer][None, :], alpha)
        h = _ffn(h, w_gate[layer].astype(BF16), w_up[layer].astype(BF16), w_down[layer].astype(BF16),
                 ln2_g[layer][None, :], ln2_b[layer][None, :], alpha)
    return h
```

```python
import functools
import math

import numpy as np
import jax
import jax.numpy as jnp
from jax import lax
from jax.experimental import pallas as pl
from jax.experimental.pallas import tpu as pltpu

F32 = jnp.float32
BF16 = jnp.bfloat16

RET_HEADS = 4
RET_HEAD_DIM = 128
RET_WIDTH = RET_HEADS * RET_HEAD_DIM
RET_CHUNK = 256
MOBA_HEADS = 8
MOBA_HEAD_DIM = 64
MOBA_WIDTH = MOBA_HEADS * MOBA_HEAD_DIM
MOBA_BLOCK = 256
MOBA_TOP_K = 3
MOBA_GROUP = 2
MOBA_KEY_LANES = 128
MOBA_WINDOW = 8
MOBA_SPLIT = 3
MOBA_AUX_ROWS = 8 + MOBA_SPLIT * MOBA_WINDOW
LOG2E = 1.4426950408889634
NORM_EPS = 1e-5
NEG_INF = -1e30

V7X_VMEM_BYTES = 64 * 1024 * 1024
LANES = 128
TOKEN_TILE = 512


def _vmem_limit(estimate_bytes):
    return int(min(V7X_VMEM_BYTES - 4 * 1024 * 1024, max(estimate_bytes, 16 * 1024 * 1024)))


def _nt_dot(a, b):
    return lax.dot_general(a, b, (((1,), (1,)), ((), ())), preferred_element_type=F32)


def _layer_norm(v, g, b):
    mu = jnp.mean(v, axis=-1, keepdims=True)
    d = v - mu
    var = jnp.mean(d * d, axis=-1, keepdims=True)
    return d * lax.rsqrt(var + NORM_EPS) * g + b


def _key_side_pattern(tm):
    hd, sp, win = MOBA_HEAD_DIM, MOBA_SPLIT, MOBA_WINDOW
    assert win % (tm // MOBA_BLOCK) == 0 and 2 * sp <= 8 and hd + MOBA_AUX_ROWS <= MOBA_KEY_LANES
    pos = np.arange(win * MOBA_BLOCK)
    pat = np.zeros((pos.size, MOBA_HEADS, MOBA_KEY_LANES), np.float32)
    pat[:, :, hd:hd + sp] = (pos % MOBA_BLOCK)[:, None, None]
    pat[:, :, hd + sp:hd + 2 * sp] = 1.0
    for s in range(sp):
        for c in range(win):
            pat[:, :, hd + 8 + s * win + c] = (pos // MOBA_BLOCK == c)[:, None]
    return jnp.asarray(pat.reshape(pos.size, MOBA_HEADS * MOBA_KEY_LANES), BF16)


def _inproj_kernel(x_ref, wn_ref, wt_ref, kpat_ref, rqkv_ref, rg_ref, mk_ref, mqt_ref, mvt_ref):
    x = x_ref[0].astype(BF16)
    nat = jnp.dot(x, wn_ref[...], preferred_element_type=F32)
    rqkv_ref[0] = nat[:, :3 * RET_WIDTH].astype(BF16)
    rg_ref[0] = nat[:, 3 * RET_WIDTH:4 * RET_WIDTH]
    mk_ref[0] = nat[:, 4 * RET_WIDTH:].astype(BF16) + kpat_ref[...]
    tr = _nt_dot(wt_ref[...], x)
    for c in range(x.shape[0] // MOBA_BLOCK):
        cols = slice(c * MOBA_BLOCK, (c + 1) * MOBA_BLOCK)
        mqt_ref[0, c] = (tr[:MOBA_WIDTH, cols] * (MOBA_HEAD_DIM ** -0.5 * LOG2E)).astype(BF16)
        mvt_ref[0, c] = tr[MOBA_WIDTH:, cols].astype(BF16)


def _inproj(x, w_nat, w_tr):
    bsz, s_len, d = x.shape
    tm = TOKEN_TILE
    nb = s_len // MOBA_BLOCK
    n_nat = w_nat.shape[1]
    k_width = MOBA_HEADS * MOBA_KEY_LANES
    kpat = _key_side_pattern(tm)
    n_var = kpat.shape[0] // tm
    est = (2 * tm * d * 4 + 2 * (w_nat.size + w_tr.size) * 2 + 2 * tm * k_width * 2
           + 2 * tm * (3 * RET_WIDTH * 2 + RET_WIDTH * 4 + (k_width + 2 * MOBA_WIDTH) * 2)
           + tm * (n_nat + 2 * MOBA_WIDTH) * 4 * 2)
    return pl.pallas_call(
        _inproj_kernel,
        grid=(bsz, s_len // tm),
        in_specs=[
            pl.BlockSpec((1, tm, d), lambda b, t: (b, t, 0)),
            pl.BlockSpec((d, n_nat), lambda b, t: (0, 0)),
            pl.BlockSpec((2 * MOBA_WIDTH, d), lambda b, t: (0, 0)),
            pl.BlockSpec((tm, k_width), lambda b, t: (t % n_var, 0)),
        ],
        out_specs=[
            pl.BlockSpec((1, tm, 3 * RET_WIDTH), lambda b, t: (b, t, 0)),
            pl.BlockSpec((1, tm, RET_WIDTH), lambda b, t: (b, t, 0)),
            pl.BlockSpec((1, tm, k_width), lambda b, t: (b, t, 0)),
            pl.BlockSpec((1, tm // MOBA_BLOCK, MOBA_WIDTH, MOBA_BLOCK), lambda b, t: (b, t, 0, 0)),
            pl.BlockSpec((1, tm // MOBA_BLOCK, MOBA_WIDTH, MOBA_BLOCK), lambda b, t: (b, t, 0, 0)),
        ],
        out_shape=[
            jax.ShapeDtypeStruct((bsz, s_len, 3 * RET_WIDTH), BF16),
            jax.ShapeDtypeStruct((bsz, s_len, RET_WIDTH), F32),
            jax.ShapeDtypeStruct((bsz, s_len, k_width), BF16),
            jax.ShapeDtypeStruct((bsz, nb, MOBA_WIDTH, MOBA_BLOCK), BF16),
            jax.ShapeDtypeStruct((bsz, nb, MOBA_WIDTH, MOBA_BLOCK), BF16),
        ],
        compiler_params=pltpu.CompilerParams(
            dimension_semantics=("parallel", "parallel"),
            vmem_limit_bytes=_vmem_limit(est)),
        name="inproj",
    )(x, w_nat, w_tr, kpat)


def _retention_tables():
    h = np.arange(RET_HEADS, dtype=np.float64)
    log_g = np.log1p(-np.exp2(-5.0 - h))
    pos = np.arange(RET_CHUNK, dtype=np.float64)
    diff = pos[:, None] - pos[None, :]
    scale = RET_HEAD_DIM ** -0.5
    intra = np.where(diff >= 0.0, np.exp(log_g[:, None, None] * np.maximum(diff, 0.0)), 0.0) * scale
    k_to_end = np.exp(log_g[:, None] * (RET_CHUNK - 1.0 - pos)[None, :]) * scale
    q_from_start = np.exp(log_g[:, None] * (pos + 1.0)[None, :])
    chunk_decay = np.exp(log_g * RET_CHUNK)
    bcast = lambda t: np.broadcast_to(t[:, :, None], (RET_HEADS, RET_CHUNK, RET_HEAD_DIM))
    return (jnp.asarray(intra, F32), jnp.asarray(bcast(k_to_end), F32),
            jnp.asarray(bcast(q_from_start), F32), jnp.asarray(chunk_decay, F32))


def _retention_kernel(cd_ref, qkv_ref, rg_ref, gain_ref, dec_ref, kte_ref, qfs_ref, o_ref, state_ref):
    e = RET_HEAD_DIM

    @pl.when(pl.program_id(1) == 0)
    def _():
        state_ref[...] = jnp.zeros_like(state_ref)

    for h in range(RET_HEADS):
        cols = lambda part: slice((part * RET_HEADS + h) * e, (part * RET_HEADS + h + 1) * e)
        q = qkv_ref[0, :, cols(0)]
        k = qkv_ref[0, :, cols(1)]
        v = qkv_ref[0, :, cols(2)]
        scores = _nt_dot(q, k) * dec_ref[h]
        intra = jnp.dot(scores.astype(BF16), v, preferred_element_type=F32)
        prev = state_ref[h]
        cross = jnp.dot(q, prev.astype(BF16), preferred_element_type=F32) * qfs_ref[h]
        y = intra + cross
        k_dec = (k.astype(F32) * kte_ref[h]).T.astype(BF16)
        kv = jnp.dot(k_dec, v, preferred_element_type=F32)
        state_ref[h] = prev * cd_ref[h] + kv
        mu = jnp.mean(y, axis=-1, keepdims=True)
        d = y - mu
        var = jnp.mean(d * d, axis=-1, keepdims=True)
        yn = d * lax.rsqrt(var + NORM_EPS) * gain_ref[:, h * e:(h + 1) * e]
        g = rg_ref[0, :, h * e:(h + 1) * e]
        o_ref[0, :, h * e:(h + 1) * e] = (g * jax.nn.sigmoid(g) * yn).astype(o_ref.dtype)


def _retention(rqkv, rg, gain, tables):
    bsz, s_len, _ = rqkv.shape
    dec, kte, qfs, cd = tables
    c, e = RET_CHUNK, RET_HEAD_DIM
    full = lambda a: pl.BlockSpec(a.shape, lambda b, n: (0,) * a.ndim)
    return pl.pallas_call(
        _retention_kernel,
        grid=(bsz, s_len // c),
        in_specs=[
            pl.BlockSpec(memory_space=pltpu.SMEM),
            pl.BlockSpec((1, c, 3 * RET_WIDTH), lambda b, n: (b, n, 0)),
            pl.BlockSpec((1, c, RET_WIDTH), lambda b, n: (b, n, 0)),
            full(gain), full(dec), full(kte), full(qfs),
        ],
        out_specs=pl.BlockSpec((1, c, RET_WIDTH), lambda b, n: (b, n, 0)),
        out_shape=jax.ShapeDtypeStruct((bsz, s_len, RET_WIDTH), BF16),
        scratch_shapes=[pltpu.VMEM((RET_HEADS, e, e), F32)],
        compiler_params=pltpu.CompilerParams(
            dimension_semantics=("parallel", "arbitrary")),
        name="retention",
    )(cd, rqkv, rg, gain, dec, kte, qfs)


def _alibi_slopes():
    slopes = np.exp2(-8.0 * (np.arange(MOBA_HEADS, dtype=np.float64) + 1.0) / MOBA_HEADS)
    return jnp.asarray(slopes, F32)


def _moba_kernel(slope_ref, qt_ref, k_ref, vt_ref, o_ref, kmean_ref, aux_ref, sa_ref, sb_ref):
    blk, hd, grp = MOBA_BLOCK, MOBA_HEAD_DIM, MOBA_GROUP
    kl, aux_rows, win, sp = MOBA_KEY_LANES, MOBA_AUX_ROWS, MOBA_WINDOW, MOBA_SPLIT
    n_b = vt_ref.shape[1]
    pair = pl.program_id(1)
    i = pl.program_id(2)

    @pl.when(i == 0)
    def _():
        for b in range(n_b):
            kb = k_ref[0, b * blk:(b + 1) * blk, :].astype(F32)
            kmean_ref[b:b + 1, :] = jnp.sum(kb, axis=0, keepdims=True) * (1.0 / blk)

    jidx = lax.broadcasted_iota(jnp.int32, (n_b, blk), 0)
    k_pos = lax.broadcasted_iota(jnp.int32, (blk, blk), 0)
    q_pos = lax.broadcasted_iota(jnp.int32, (blk, blk), 1)
    row8 = lax.broadcasted_iota(jnp.int32, (8, blk), 0)
    r_q8 = lax.broadcasted_iota(jnp.int32, (8, blk), 1).astype(F32)
    past = jidx < i
    q_tops = [qt_ref[0, 0, hl * hd:(hl + 1) * hd, :] for hl in range(2)]
    zero_rows = lambda n: jnp.zeros((n, blk), BF16)

    def bf16_pieces(v):
        pieces = []
        for _ in range(sp):
            piece = v.astype(BF16).astype(F32)
            pieces.append(piece)
            v = v - piece
        return pieces

    def q_aug(hl, aux):
        return jnp.concatenate([q_tops[hl], aux, zero_rows(kl - hd - aux_rows)], axis=0)

    def raw_scores(hl, t):
        start = t * (grp * blk)
        if not isinstance(t, int):
            start = pl.multiple_of(start, grp * blk)
        k_rows = k_ref[0, pl.ds(start, grp * blk), hl * kl:(hl + 1) * kl]
        return jnp.dot(k_rows, q_aug(hl, aux_ref[hl, (t * grp) // win]), preferred_element_type=F32)

    def tile_values(hl, t):
        return jnp.concatenate(
            [vt_ref[0, t * grp + c, hl * hd:(hl + 1) * hd, :] for c in range(grp)], axis=1)

    def col_max(u3):
        return jnp.max(jnp.max(u3, axis=0), axis=0, keepdims=True)

    def col_sum(p3):
        return jnp.sum(jnp.sum(p3, axis=0), axis=0, keepdims=True)

    def consume(hl, s_ref, t, m, l, acc):
        u = s_ref[hl].reshape(grp, blk, blk)
        m_new = jnp.maximum(m, col_max(u))
        alpha = jnp.exp2(m - m_new)
        p = jnp.exp2(u - m_new[None])
        l = alpha * l + col_sum(p)
        acc = alpha * acc + jnp.dot(tile_values(hl, t), p.reshape(grp * blk, blk).astype(BF16),
                                    preferred_element_type=F32)
        return m_new, l, acc

    state = []
    for hl in range(2):
        slope = slope_ref[2 * pair + hl]
        kmean = kmean_ref[:, hl * kl:(hl + 1) * kl].astype(BF16)
        q_gate = jnp.concatenate([q_tops[hl], zero_rows(kl - hd)], axis=0)
        gate = jnp.dot(kmean, q_gate, preferred_element_type=F32)
        taken = jnp.zeros((n_b, blk), jnp.bool_)
        for _ in range(MOBA_TOP_K):
            cand = past & jnp.logical_not(taken)
            best = jnp.max(jnp.where(cand, gate, -jnp.inf), axis=0, keepdims=True)
            pick = cand & (gate == best)
            first = jnp.min(jnp.where(pick, jidx, n_b), axis=0, keepdims=True)
            taken = taken | (jidx == first)
        c = slope * LOG2E
        block_off = (i - jidx).astype(F32) * (-float(blk) * c)
        bias_pieces = bf16_pieces(jnp.where(taken, block_off, NEG_INF))
        alibi = jnp.where(row8 < sp, c, -c * r_q8)
        alibi_pieces = bf16_pieces(alibi)
        alibi_rows = jnp.zeros((8, blk), F32)
        for s in range(sp):
            alibi_rows = jnp.where((row8 == s) | (row8 == sp + s), alibi_pieces[s], alibi_rows)
        for w in range(n_b // win):
            rows = [alibi_rows] + [piece[w * win:(w + 1) * win, :] for piece in bias_pieces]
            aux_ref[hl, w] = jnp.concatenate(rows, axis=0).astype(BF16)
        aux_own = jnp.concatenate([alibi_rows.astype(BF16), zero_rows(aux_rows - 8)], axis=0)

        kd = k_ref[0, pl.ds(pl.multiple_of(i * blk, blk), blk), hl * kl:(hl + 1) * kl]
        ud = jnp.where(k_pos <= q_pos,
                       jnp.dot(kd, q_aug(hl, aux_own), preferred_element_type=F32), NEG_INF)
        m = jnp.max(ud, axis=0, keepdims=True)
        pd = jnp.exp2(ud - m)
        l = jnp.sum(pd, axis=0, keepdims=True)
        acc = jnp.dot(vt_ref[0, i, hl * hd:(hl + 1) * hd, :], pd.astype(BF16),
                      preferred_element_type=F32)
        state += [m, l, acc]
        sa_ref[hl] = raw_scores(hl, 0)

    last_tile = n_b // grp - 1

    def body(tt, carry):
        carry = list(carry)
        t0 = 2 * tt
        for hl in range(2):
            sb_ref[hl] = raw_scores(hl, t0 + 1)
        for hl in range(2):
            carry[3 * hl:3 * hl + 3] = consume(hl, sa_ref, t0, *carry[3 * hl:3 * hl + 3])
        for hl in range(2):
            sa_ref[hl] = raw_scores(hl, jnp.minimum(t0 + 2, last_tile))
        for hl in range(2):
            carry[3 * hl:3 * hl + 3] = consume(hl, sb_ref, t0 + 1, *carry[3 * hl:3 * hl + 3])
        return tuple(carry)

    n_tiles = (i + grp - 1) // grp
    state = lax.fori_loop(0, (n_tiles + 1) // 2, body, tuple(state))
    outs = [state[3 * hl + 2] * (1.0 / state[3 * hl + 1]) for hl in range(2)]
    o_ref[0] = jnp.concatenate(outs, axis=0).T.astype(o_ref.dtype)


def _moba(mqt, mk, mvt, slopes):
    bsz, n_b, width, blk = mqt.shape
    s_len = mk.shape[1]
    pw = 2 * MOBA_HEAD_DIM
    kw = 2 * MOBA_KEY_LANES
    tile = MOBA_GROUP * blk
    est = 2 * s_len * (kw + pw) * 2 + 2 * 2 * tile * blk * 4 + 2 * 4 * tile * blk * 4
    return pl.pallas_call(
        _moba_kernel,
        grid=(bsz, width // pw, n_b),
        in_specs=[
            pl.BlockSpec(memory_space=pltpu.SMEM),
            pl.BlockSpec((1, 1, pw, blk), lambda b, p, i: (b, i, p, 0)),
            pl.BlockSpec((1, s_len, kw), lambda b, p, i: (b, 0, p)),
            pl.BlockSpec((1, n_b, pw, blk), lambda b, p, i: (b, 0, p, 0)),
        ],
        out_specs=pl.BlockSpec((1, blk, pw), lambda b, p, i: (b, i, p)),
        out_shape=jax.ShapeDtypeStruct((bsz, s_len, width), BF16),
        scratch_shapes=[pltpu.VMEM((n_b, kw), F32),
                        pltpu.VMEM((2, n_b // MOBA_WINDOW, MOBA_AUX_ROWS, blk), BF16),
                        pltpu.VMEM((2, tile, blk), F32),
                        pltpu.VMEM((2, tile, blk), F32)],
        compiler_params=pltpu.CompilerParams(
            dimension_semantics=("parallel", "parallel", "arbitrary"),
            vmem_limit_bytes=_vmem_limit(est)),
        name="moba",
    )(slopes, mqt, mk, mvt)


def _outproj_kernel(alpha, x_ref, ret_ref, moba_ref, wr_ref, wm_ref, g_ref, b_ref, o_ref):
    mixed = (jnp.dot(ret_ref[0], wr_ref[...], preferred_element_type=F32)
             + jnp.dot(moba_ref[0], wm_ref[...], preferred_element_type=F32))
    o_ref[0] = _layer_norm(alpha * x_ref[0] + mixed, g_ref[...], b_ref[...])


def _outproj(x, ret, moba, w_ret, w_moba, g, b, alpha):
    bsz, s_len, d = x.shape
    tm = TOKEN_TILE
    row = lambda w: pl.BlockSpec((1, tm, w), lambda b_, t: (b_, t, 0))
    full = lambda a: pl.BlockSpec(a.shape, lambda b_, t: (0, 0))
    est = 2 * (2 * tm * d * 4 + 2 * tm * RET_WIDTH * 2 + 2 * d * d * 2) + 4 * tm * d * 4
    return pl.pallas_call(
        functools.partial(_outproj_kernel, alpha),
        grid=(bsz, s_len // tm),
        in_specs=[row(d), row(RET_WIDTH), row(MOBA_WIDTH), full(w_ret), full(w_moba), full(g), full(b)],
        out_specs=row(d),
        out_shape=jax.ShapeDtypeStruct((bsz, s_len, d), F32),
        compiler_params=pltpu.CompilerParams(
            dimension_semantics=("parallel", "parallel"),
            vmem_limit_bytes=_vmem_limit(est)),
        name="outproj_ln1",
    )(x, ret, moba, w_ret, w_moba, g, b)


def _ffn_kernel(alpha, h_ref, wg_ref, wu_ref, wd_ref, g_ref, b_ref, o_ref):
    h = h_ref[0]
    hb = h.astype(BF16)
    gate = jnp.dot(hb, wg_ref[...], preferred_element_type=F32)
    up = jnp.dot(hb, wu_ref[...], preferred_element_type=F32)
    act = (gate * jax.nn.sigmoid(gate) * up).astype(BF16)
    ffn = jnp.dot(act, wd_ref[...], preferred_element_type=F32)
    o_ref[0] = _layer_norm(alpha * h + ffn, g_ref[...], b_ref[...])


def _ffn(h, w_gate, w_up, w_down, g, b, alpha):
    bsz, s_len, d = h.shape
    d_ff = w_gate.shape[1]
    tm = TOKEN_TILE
    row = pl.BlockSpec((1, tm, d), lambda b_, t: (b_, t, 0))
    full = lambda a: pl.BlockSpec(a.shape, lambda b_, t: (0, 0), pipeline_mode=pl.Buffered(1))
    est = 3 * d * d_ff * 2 + 4 * tm * d * 4 + tm * d_ff * (4 + 4 + 2) + 2 * tm * d * 4
    return pl.pallas_call(
        functools.partial(_ffn_kernel, alpha),
        grid=(bsz, s_len // tm),
        in_specs=[row, full(w_gate), full(w_up), full(w_down), full(g), full(b)],
        out_specs=row,
        out_shape=jax.ShapeDtypeStruct((bsz, s_len, d), F32),
        compiler_params=pltpu.CompilerParams(
            dimension_semantics=("parallel", "parallel"),
            vmem_limit_bytes=_vmem_limit(est)),
        name="ffn_ln2",
    )(h, w_gate, w_up, w_down, g, b)


def kernel(x, w_in, ret_gn_gain, w_out, ln1_g, ln1_b, w_gate, w_up, w_down, ln2_g, ln2_b):
    bsz, s_len, d = x.shape
    depth = w_in.shape[0]
    assert s_len % math.lcm(RET_CHUNK, MOBA_BLOCK, TOKEN_TILE) == 0
    assert w_in.shape[2] == 4 * RET_WIDTH + 3 * MOBA_WIDTH
    alpha = (2 * depth) ** 0.25
    ret_tables = _retention_tables()
    slopes = _alibi_slopes()
    r4 = 4 * RET_WIDTH
    h = x
    for layer in range(depth):
        w = w_in[layer]
        w_mk = w[:, r4 + MOBA_WIDTH:r4 + 2 * MOBA_WIDTH].reshape(d, MOBA_HEADS, MOBA_HEAD_DIM)
        w_mk = jnp.pad(w_mk, ((0, 0), (0, 0), (0, MOBA_KEY_LANES - MOBA_HEAD_DIM)))
        w_nat = jnp.concatenate([w[:, :r4], w_mk.reshape(d, MOBA_HEADS * MOBA_KEY_LANES)], axis=1).astype(BF16)
        w_tr = jnp.concatenate([w[:, r4:r4 + MOBA_WIDTH], w[:, r4 + 2 * MOBA_WIDTH:]], axis=1).T.astype(BF16)
        rqkv, rg, mk, mqt, mvt = _inproj(h, w_nat, w_tr)
        ret = _retention(rqkv, rg, ret_gn_gain[layer][None, :], ret_tables)
        moba = _moba(mqt, mk, mvt, slopes)
        wo = w_out[layer].astype(BF16)
        h = _outproj(h, ret, moba, wo[:RET_WIDTH], wo[RET_WIDTH:],
                     ln1_g[layer][None, :], ln1_b[layer][None, :], alpha)
        h = _ffn(h, w_gate[layer].astype(BF16), w_up[layer].astype(BF16), w_down[layer].astype(BF16),
                 ln2_g[layer][None, :], ln2_b[layer][None, :], alpha)
    return h
```

```python
import functools
import math

import numpy as np
import jax
import jax.numpy as jnp
from jax import lax
from jax.experimental import pallas as pl
from jax.experimental.pallas import tpu as pltpu

F32 = jnp.float32
BF16 = jnp.bfloat16

RET_HEADS = 4
RET_HEAD_DIM = 128
RET_WIDTH = RET_HEADS * RET_HEAD_DIM
RET_CHUNK = 256
MOBA_HEADS = 8
MOBA_HEAD_DIM = 64
MOBA_WIDTH = MOBA_HEADS * MOBA_HEAD_DIM
MOBA_BLOCK = 256
MOBA_TOP_K = 3
MOBA_GROUP = 2
MOBA_KEY_LANES = 128
MOBA_WINDOW = 8
MOBA_SPLIT = 3
MOBA_AUX_ROWS = 8 + MOBA_SPLIT * MOBA_WINDOW
LOG2E = 1.4426950408889634
NORM_EPS = 1e-5
NEG_INF = -1e30

V7X_VMEM_BYTES = 64 * 1024 * 1024
LANES = 128
TOKEN_TILE = 512


def _vmem_limit(estimate_bytes):
    return int(min(V7X_VMEM_BYTES - 4 * 1024 * 1024, max(estimate_bytes, 16 * 1024 * 1024)))


def _nt_dot(a, b):
    return lax.dot_general(a, b, (((1,), (1,)), ((), ())), preferred_element_type=F32)


def _layer_norm(v, g, b):
    mu = jnp.mean(v, axis=-1, keepdims=True)
    d = v - mu
    var = jnp.mean(d * d, axis=-1, keepdims=True)
    return d * lax.rsqrt(var + NORM_EPS) * g + b


def _key_side_pattern(tm):
    hd, sp, win = MOBA_HEAD_DIM, MOBA_SPLIT, MOBA_WINDOW
    assert win % (tm // MOBA_BLOCK) == 0 and 2 * sp <= 8 and hd + MOBA_AUX_ROWS <= MOBA_KEY_LANES
    pos = np.arange(win * MOBA_BLOCK)
    pat = np.zeros((pos.size, MOBA_HEADS, MOBA_KEY_LANES), np.float32)
    pat[:, :, hd:hd + sp] = (pos % MOBA_BLOCK)[:, None, None]
    pat[:, :, hd + sp:hd + 2 * sp] = 1.0
    for s in range(sp):
        for c in range(win):
            pat[:, :, hd + 8 + s * win + c] = (pos // MOBA_BLOCK == c)[:, None]
    return jnp.asarray(pat.reshape(pos.size, MOBA_HEADS * MOBA_KEY_LANES), BF16)


def _inproj_kernel(x_ref, wn_ref, wt_ref, kpat_ref, rqkv_ref, rg_ref, mk_ref, mqt_ref, mvt_ref):
    x = x_ref[0].astype(BF16)
    nat = jnp.dot(x, wn_ref[...], preferred_element_type=F32)
    rqkv_ref[0] = nat[:, :3 * RET_WIDTH].astype(BF16)
    rg_ref[0] = nat[:, 3 * RET_WIDTH:4 * RET_WIDTH]
    mk_ref[0] = nat[:, 4 * RET_WIDTH:].astype(BF16) + kpat_ref[...]
    tr = _nt_dot(wt_ref[...], x)
    for c in range(x.shape[0] // MOBA_BLOCK):
        cols = slice(c * MOBA_BLOCK, (c + 1) * MOBA_BLOCK)
        mqt_ref[0, c] = (tr[:MOBA_WIDTH, cols] * (MOBA_HEAD_DIM ** -0.5 * LOG2E)).astype(BF16)
        mvt_ref[0, c] = tr[MOBA_WIDTH:, cols].astype(BF16)


def _inproj(x, w_nat, w_tr):
    bsz, s_len, d = x.shape
    tm = TOKEN_TILE
    nb = s_len // MOBA_BLOCK
    n_nat = w_nat.shape[1]
    k_width = MOBA_HEADS * MOBA_KEY_LANES
    kpat = _key_side_pattern(tm)
    n_var = kpat.shape[0] // tm
    est = (2 * tm * d * 4 + 2 * (w_nat.size + w_tr.size) * 2 + 2 * tm * k_width * 2
           + 2 * tm * (3 * RET_WIDTH * 2 + RET_WIDTH * 4 + (k_width + 2 * MOBA_WIDTH) * 2)
           + tm * (n_nat + 2 * MOBA_WIDTH) * 4 * 2)
    return pl.pallas_call(
        _inproj_kernel,
        grid=(bsz, s_len // tm),
        in_specs=[
            pl.BlockSpec((1, tm, d), lambda b, t: (b, t, 0)),
            pl.BlockSpec((d, n_nat), lambda b, t: (0, 0)),
            pl.BlockSpec((2 * MOBA_WIDTH, d), lambda b, t: (0, 0)),
            pl.BlockSpec((tm, k_width), lambda b, t: (t % n_var, 0)),
        ],
        out_specs=[
            pl.BlockSpec((1, tm, 3 * RET_WIDTH), lambda b, t: (b, t, 0)),
            pl.BlockSpec((1, tm, RET_WIDTH), lambda b, t: (b, t, 0)),
            pl.BlockSpec((1, tm, k_width), lambda b, t: (b, t, 0)),
            pl.BlockSpec((1, tm // MOBA_BLOCK, MOBA_WIDTH, MOBA_BLOCK), lambda b, t: (b, t, 0, 0)),
            pl.BlockSpec((1, tm // MOBA_BLOCK, MOBA_WIDTH, MOBA_BLOCK), lambda b, t: (b, t, 0, 0)),
        ],
        out_shape=[
            jax.ShapeDtypeStruct((bsz, s_len, 3 * RET_WIDTH), BF16),
            jax.ShapeDtypeStruct((bsz, s_len, RET_WIDTH), F32),
            jax.ShapeDtypeStruct((bsz, s_len, k_width), BF16),
            jax.ShapeDtypeStruct((bsz, nb, MOBA_WIDTH, MOBA_BLOCK), BF16),
            jax.ShapeDtypeStruct((bsz, nb, MOBA_WIDTH, MOBA_BLOCK), BF16),
        ],
        compiler_params=pltpu.CompilerParams(
            dimension_semantics=("parallel", "parallel"),
            vmem_limit_bytes=_vmem_limit(est)),
        name="inproj",
    )(x, w_nat, w_tr, kpat)


def _retention_tables():
    h = np.arange(RET_HEADS, dtype=np.float64)
    log_g = np.log1p(-np.exp2(-5.0 - h))
    pos = np.arange(RET_CHUNK, dtype=np.float64)
    diff = pos[:, None] - pos[None, :]
    scale = RET_HEAD_DIM ** -0.5
    intra = np.where(diff >= 0.0, np.exp(log_g[:, None, None] * np.maximum(diff, 0.0)), 0.0) * scale
    k_to_end = np.exp(log_g[:, None] * (RET_CHUNK - 1.0 - pos)[None, :]) * scale
    q_from_start = np.exp(log_g[:, None] * (pos + 1.0)[None, :])
    chunk_decay = np.exp(log_g * RET_CHUNK)
    bcast = lambda t: np.broadcast_to(t[:, :, None], (RET_HEADS, RET_CHUNK, RET_HEAD_DIM))
    return (jnp.asarray(intra, F32), jnp.asarray(bcast(k_to_end), F32),
            jnp.asarray(bcast(q_from_start), F32), jnp.asarray(chunk_decay, F32))


def _retention_kernel(cd_ref, qkv_ref, rg_ref, gain_ref, dec_ref, kte_ref, qfs_ref, o_ref, state_ref):
    e = RET_HEAD_DIM

    @pl.when(pl.program_id(1) == 0)
    def _():
        state_ref[...] = jnp.zeros_like(state_ref)

    for h in range(RET_HEADS):
        cols = lambda part: slice((part * RET_HEADS + h) * e, (part * RET_HEADS + h + 1) * e)
        q = qkv_ref[0, :, cols(0)]
        k = qkv_ref[0, :, cols(1)]
        v = qkv_ref[0, :, cols(2)]
        scores = _nt_dot(q, k) * dec_ref[h]
        intra = jnp.dot(scores.astype(BF16), v, preferred_element_type=F32)
        prev = state_ref[h]
        cross = jnp.dot(q, prev.astype(BF16), preferred_element_type=F32) * qfs_ref[h]
        y = intra + cross
        k_dec = (k.astype(F32) * kte_ref[h]).T.astype(BF16)
        kv = jnp.dot(k_dec, v, preferred_element_type=F32)
        state_ref[h] = prev * cd_ref[h] + kv
        mu = jnp.mean(y, axis=-1, keepdims=True)
        d = y - mu
        var = jnp.mean(d * d, axis=-1, keepdims=True)
        yn = d * lax.rsqrt(var + NORM_EPS) * gain_ref[:, h * e:(h + 1) * e]
        g = rg_ref[0, :, h * e:(h + 1) * e]
        o_ref[0, :, h * e:(h + 1) * e] = (g * jax.nn.sigmoid(g) * yn).astype(o_ref.dtype)


def _retention(rqkv, rg, gain, tables):
    bsz, s_len, _ = rqkv.shape
    dec, kte, qfs, cd = tables
    c, e = RET_CHUNK, RET_HEAD_DIM
    full = lambda a: pl.BlockSpec(a.shape, lambda b, n: (0,) * a.ndim)
    return pl.pallas_call(
        _retention_kernel,
        grid=(bsz, s_len // c),
        in_specs=[
            pl.BlockSpec(memory_space=pltpu.SMEM),
            pl.BlockSpec((1, c, 3 * RET_WIDTH), lambda b, n: (b, n, 0)),
            pl.BlockSpec((1, c, RET_WIDTH), lambda b, n: (b, n, 0)),
            full(gain), full(dec), full(kte), full(qfs),
        ],
        out_specs=pl.BlockSpec((1, c, RET_WIDTH), lambda b, n: (b, n, 0)),
        out_shape=jax.ShapeDtypeStruct((bsz, s_len, RET_WIDTH), BF16),
        scratch_shapes=[pltpu.VMEM((RET_HEADS, e, e), F32)],
        compiler_params=pltpu.CompilerParams(
            dimension_semantics=("parallel", "arbitrary")),
        name="retention",
    )(cd, rqkv, rg, gain, dec, kte, qfs)


def _alibi_slopes():
    slopes = np.exp2(-8.0 * (np.arange(MOBA_HEADS, dtype=np.float64) + 1.0) / MOBA_HEADS)
    return jnp.asarray(slopes, F32)


def _moba_past_tiles(n_b):
    ceil_div = lambda a: -(-a // MOBA_GROUP)
    totals = {ceil_div(s) + ceil_div(n_b - 1 - s) for s in range(n_b // 2)}
    assert len(totals) == 1 and n_b % 2 == 0
    return totals.pop()


def _moba_kernel(slope_ref, qa_ref, qb_ref, k_ref, vt_ref, oa_ref, ob_ref,
                 kmean_ref, qtop_ref, aux_ref, m_ref, l_ref, acc_ref):
    blk, hd, grp = MOBA_BLOCK, MOBA_HEAD_DIM, MOBA_GROUP
    kl, aux_rows, win, sp = MOBA_KEY_LANES, MOBA_AUX_ROWS, MOBA_WINDOW, MOBA_SPLIT
    tile = grp * blk
    n_b = vt_ref.shape[1]
    n_past = _moba_past_tiles(n_b)
    pair = pl.program_id(1)
    step = pl.program_id(2)
    own = (step, n_b - 1 - step)
    n_past_first = (step + grp - 1) // grp
    q_refs, o_refs = (qa_ref, qb_ref), (oa_ref, ob_ref)

    @pl.when(step == 0)
    def _():
        for b in range(n_b):
            kb = k_ref[0, b * blk:(b + 1) * blk, :].astype(F32)
            kmean_ref[b:b + 1, :] = jnp.sum(kb, axis=0, keepdims=True) * (1.0 / blk)

    jidx = lax.broadcasted_iota(jnp.int32, (n_b, blk), 0)
    causal = (lax.broadcasted_iota(jnp.int32, (blk, blk), 0)
              <= lax.broadcasted_iota(jnp.int32, (blk, blk), 1))
    row8 = lax.broadcasted_iota(jnp.int32, (8, blk), 0)
    r_q8 = lax.broadcasted_iota(jnp.int32, (8, blk), 1).astype(F32)
    zero_rows = lambda n: jnp.zeros((n, blk), BF16)

    def bf16_pieces(v):
        pieces = []
        for _ in range(sp):
            piece = v.astype(BF16).astype(F32)
            pieces.append(piece)
            v = v - piece
        return pieces

    def col_max(u):
        return jnp.max(jnp.max(u.reshape(-1, blk, blk), axis=0), axis=0, keepdims=True)

    def col_sum(p):
        return jnp.sum(jnp.sum(p.reshape(-1, blk, blk), axis=0), axis=0, keepdims=True)

    alibi_rows = []
    for hl in range(2):
        c = slope_ref[2 * pair + hl] * LOG2E
        pieces = bf16_pieces(jnp.where(row8 < sp, c, -c * r_q8))
        rows = jnp.zeros((8, blk), F32)
        for s in range(sp):
            rows = jnp.where((row8 == s) | (row8 == sp + s), pieces[s], rows)
        alibi_rows.append(rows)

    chains = [(o, hl) for o in range(2) for hl in range(2)]
    own_scores, gates = {}, {}
    for o, hl in chains:
        q_top = q_refs[o][0, 0, hl * hd:(hl + 1) * hd, :]
        qtop_ref[o, hl] = q_top
        aux_own = jnp.concatenate([alibi_rows[hl], jnp.zeros((aux_rows - 8, blk), F32)], axis=0)
        q_own = jnp.concatenate([q_top, aux_own.astype(BF16), zero_rows(kl - hd - aux_rows)], axis=0)
        kd = k_ref[0, pl.ds(pl.multiple_of(own[o] * blk, blk), blk), hl * kl:(hl + 1) * kl]
        own_scores[o, hl] = jnp.dot(kd, q_own, preferred_element_type=F32)
        kmean = kmean_ref[:, hl * kl:(hl + 1) * kl].astype(BF16)
        q_gate = jnp.concatenate([q_top, zero_rows(kl - hd)], axis=0)
        gates[o, hl] = jnp.dot(kmean, q_gate, preferred_element_type=F32)

    for o, hl in chains:
        u = jnp.where(causal, own_scores[o, hl], NEG_INF)
        m = col_max(u)
        p = jnp.exp2(u - m)
        m_ref[o, hl] = m
        l_ref[o, hl] = col_sum(p)
        acc_ref[o, hl] = jnp.dot(vt_ref[0, own[o], hl * hd:(hl + 1) * hd, :], p.astype(BF16),
                                 preferred_element_type=F32)

        past = jidx < own[o]
        gate = gates[o, hl]
        taken = jnp.zeros((n_b, blk), jnp.bool_)
        for _ in range(MOBA_TOP_K):
            cand = past & jnp.logical_not(taken)
            best = jnp.max(jnp.where(cand, gate, -jnp.inf), axis=0, keepdims=True)
            pick = cand & (gate == best)
            first = jnp.min(jnp.where(pick, jidx, n_b), axis=0, keepdims=True)
            taken = taken | (jidx == first)
        c = slope_ref[2 * pair + hl] * LOG2E
        block_off = (own[o] - jidx).astype(F32) * (-float(blk) * c)
        bias_pieces = bf16_pieces(jnp.where(taken, block_off, NEG_INF))
        for w in range(n_b // win):
            rows = [alibi_rows[hl]] + [piece[w * win:(w + 1) * win, :] for piece in bias_pieces]
            aux_ref[o, hl, w] = jnp.concatenate(rows, axis=0).astype(BF16)

    def tile_of(k):
        which = jnp.where(k >= n_past_first, 1, 0)
        return which, k - which * n_past_first

    def scores(k, hl):
        which, t = tile_of(k)
        q_op = jnp.concatenate([qtop_ref[which, hl], aux_ref[which, hl, (t * grp) // win],
                                zero_rows(kl - hd - aux_rows)], axis=0)
        k_rows = k_ref[0, pl.ds(pl.multiple_of(t * tile, tile), tile), hl * kl:(hl + 1) * kl]
        return jnp.dot(k_rows, q_op, preferred_element_type=F32)

    def scores_and_max(k):
        s = [scores(k, hl) for hl in range(2)]
        return s, [col_max(s_hl) for s_hl in s]

    s_next, cmax_next = scores_and_max(0)
    for k in range(n_past):
        s_cur, cmax_cur = s_next, cmax_next
        if k + 1 < n_past:
            s_next, cmax_next = scores_and_max(k + 1)
        which, t = tile_of(k)
        for hl in range(2):
            u = s_cur[hl]
            m = m_ref[which, hl]
            m_new = jnp.maximum(m, cmax_cur[hl])
            alpha = jnp.exp2(m - m_new)
            p = jnp.exp2(u - m_new)
            m_ref[which, hl] = m_new
            l_ref[which, hl] = alpha * l_ref[which, hl] + col_sum(p)
            vt = jnp.concatenate(
                [vt_ref[0, t * grp + c, hl * hd:(hl + 1) * hd, :] for c in range(grp)], axis=1)
            acc_ref[which, hl] = alpha * acc_ref[which, hl] + jnp.dot(
                vt, p.astype(BF16), preferred_element_type=F32)

    for o in range(2):
        outs = [acc_ref[o, hl] * (1.0 / l_ref[o, hl]) for hl in range(2)]
        o_refs[o][0] = jnp.concatenate(outs, axis=0).T.astype(oa_ref.dtype)


def _moba(mqt, mk, mvt, slopes):
    bsz, n_b, width, blk = mqt.shape
    s_len = mk.shape[1]
    pw = 2 * MOBA_HEAD_DIM
    kw = 2 * MOBA_KEY_LANES
    tile = MOBA_GROUP * blk
    half = n_b // 2
    est = 2 * s_len * (kw + pw) * 2 + 4 * 2 * 4 * tile * blk * 4
    out_half = jax.ShapeDtypeStruct((bsz, half * blk, width), BF16)
    lo, hi = pl.pallas_call(
        _moba_kernel,
        grid=(bsz, width // pw, half),
        in_specs=[
            pl.BlockSpec(memory_space=pltpu.SMEM),
            pl.BlockSpec((1, 1, pw, blk), lambda b, p, s: (b, s, p, 0)),
            pl.BlockSpec((1, 1, pw, blk), lambda b, p, s: (b, n_b - 1 - s, p, 0)),
            pl.BlockSpec((1, s_len, kw), lambda b, p, s: (b, 0, p)),
            pl.BlockSpec((1, n_b, pw, blk), lambda b, p, s: (b, 0, p, 0)),
        ],
        out_specs=[pl.BlockSpec((1, blk, pw), lambda b, p, s: (b, s, p)),
                   pl.BlockSpec((1, blk, pw), lambda b, p, s: (b, half - 1 - s, p))],
        out_shape=[out_half, out_half],
        scratch_shapes=[pltpu.VMEM((n_b, kw), F32),
                        pltpu.VMEM((2, 2, MOBA_HEAD_DIM, blk), BF16),
                        pltpu.VMEM((2, 2, n_b // MOBA_WINDOW, MOBA_AUX_ROWS, blk), BF16),
                        pltpu.VMEM((2, 2, 1, blk), F32),
                        pltpu.VMEM((2, 2, 1, blk), F32),
                        pltpu.VMEM((2, 2, MOBA_HEAD_DIM, blk), F32)],
        compiler_params=pltpu.CompilerParams(
            dimension_semantics=("parallel", "parallel", "arbitrary"),
            vmem_limit_bytes=_vmem_limit(est)),
        name="moba",
    )(slopes, mqt, mqt, mk, mvt)
    return jnp.concatenate([lo, hi], axis=1)


def _outproj_kernel(alpha, x_ref, ret_ref, moba_ref, wr_ref, wm_ref, g_ref, b_ref, o_ref):
    mixed = (jnp.dot(ret_ref[0], wr_ref[...], preferred_element_type=F32)
             + jnp.dot(moba_ref[0], wm_ref[...], preferred_element_type=F32))
    o_ref[0] = _layer_norm(alpha * x_ref[0] + mixed, g_ref[...], b_ref[...])


def _outproj(x, ret, moba, w_ret, w_moba, g, b, alpha):
    bsz, s_len, d = x.shape
    tm = TOKEN_TILE
    row = lambda w: pl.BlockSpec((1, tm, w), lambda b_, t: (b_, t, 0))
    full = lambda a: pl.BlockSpec(a.shape, lambda b_, t: (0, 0))
    est = 2 * (2 * tm * d * 4 + 2 * tm * RET_WIDTH * 2 + 2 * d * d * 2) + 4 * tm * d * 4
    return pl.pallas_call(
        functools.partial(_outproj_kernel, alpha),
        grid=(bsz, s_len // tm),
        in_specs=[row(d), row(RET_WIDTH), row(MOBA_WIDTH), full(w_ret), full(w_moba), full(g), full(b)],
        out_specs=row(d),
        out_shape=jax.ShapeDtypeStruct((bsz, s_len, d), F32),
        compiler_params=pltpu.CompilerParams(
            dimension_semantics=("parallel", "parallel"),
            vmem_limit_bytes=_vmem_limit(est)),
        name="outproj_ln1",
    )(x, ret, moba, w_ret, w_moba, g, b)


def _ffn_kernel(alpha, h_ref, wg_ref, wu_ref, wd_ref, g_ref, b_ref, o_ref):
    h = h_ref[0]
    hb = h.astype(BF16)
    gate = jnp.dot(hb, wg_ref[...], preferred_element_type=F32)
    up = jnp.dot(hb, wu_ref[...], preferred_element_type=F32)
    act = (gate * jax.nn.sigmoid(gate) * up).astype(BF16)
    ffn = jnp.dot(act, wd_ref[...], preferred_element_type=F32)
    o_ref[0] = _layer_norm(alpha * h + ffn, g_ref[...], b_ref[...])


def _ffn(h, w_gate, w_up, w_down, g, b, alpha):
    bsz, s_len, d = h.shape
    d_ff = w_gate.shape[1]
    tm = TOKEN_TILE
    row = pl.BlockSpec((1, tm, d), lambda b_, t: (b_, t, 0))
    full = lambda a: pl.BlockSpec(a.shape, lambda b_, t: (0, 0), pipeline_mode=pl.Buffered(1))
    est = 3 * d * d_ff * 2 + 4 * tm * d * 4 + tm * d_ff * (4 + 4 + 2) + 2 * tm * d * 4
    return pl.pallas_call(
        functools.partial(_ffn_kernel, alpha),
        grid=(bsz, s_len // tm),
        in_specs=[row, full(w_gate), full(w_up), full(w_down), full(g), full(b)],
        out_specs=row,
        out_shape=jax.ShapeDtypeStruct((bsz, s_len, d), F32),
        compiler_params=pltpu.CompilerParams(
            dimension_semantics=("parallel", "parallel"),
            vmem_limit_bytes=_vmem_limit(est)),
        name="ffn_ln2",
    )(h, w_gate, w_up, w_down, g, b)


def kernel(x, w_in, ret_gn_gain, w_out, ln1_g, ln1_b, w_gate, w_up, w_down, ln2_g, ln2_b):
    bsz, s_len, d = x.shape
    depth = w_in.shape[0]
    assert s_len % math.lcm(RET_CHUNK, MOBA_BLOCK, TOKEN_TILE) == 0
    assert w_in.shape[2] == 4 * RET_WIDTH + 3 * MOBA_WIDTH
    alpha = (2 * depth) ** 0.25
    ret_tables = _retention_tables()
    slopes = _alibi_slopes()
    r4 = 4 * RET_WIDTH
    h = x
    for layer in range(depth):
        w = w_in[layer]
        w_mk = w[:, r4 + MOBA_WIDTH:r4 + 2 * MOBA_WIDTH].reshape(d, MOBA_HEADS, MOBA_HEAD_DIM)
        w_mk = jnp.pad(w_mk, ((0, 0), (0, 0), (0, MOBA_KEY_LANES - MOBA_HEAD_DIM)))
        w_nat = jnp.concatenate([w[:, :r4], w_mk.reshape(d, MOBA_HEADS * MOBA_KEY_LANES)], axis=1).astype(BF16)
        w_tr = jnp.concatenate([w[:, r4:r4 + MOBA_WIDTH], w[:, r4 + 2 * MOBA_WIDTH:]], axis=1).T.astype(BF16)
        rqkv, rg, mk, mqt, mvt = _inproj(h, w_nat, w_tr)
        ret = _retention(rqkv, rg, ret_gn_gain[layer][None, :], ret_tables)
        moba = _moba(mqt, mk, mvt, slopes)
        wo = w_out[layer].astype(BF16)
        h = _outproj(h, ret, moba, wo[:RET_WIDTH], wo[RET_WIDTH:],
                     ln1_g[layer][None, :], ln1_b[layer][None, :], alpha)
        h = _ffn(h, w_gate[layer].astype(BF16), w_up[layer].astype(BF16), w_down[layer].astype(BF16),
                 ln2_g[layer][None, :], ln2_b[layer][None, :], alpha)
    return h
```

```python
import functools
import math

import numpy as np
import jax
import jax.numpy as jnp
from jax import lax
from jax.experimental import pallas as pl
from jax.experimental.pallas import tpu as pltpu

F32 = jnp.float32
BF16 = jnp.bfloat16

RET_HEADS = 4
RET_HEAD_DIM = 128
RET_WIDTH = RET_HEADS * RET_HEAD_DIM
RET_CHUNK = 256
MOBA_HEADS = 8
MOBA_HEAD_DIM = 64
MOBA_WIDTH = MOBA_HEADS * MOBA_HEAD_DIM
MOBA_BLOCK = 256
MOBA_TOP_K = 3
MOBA_GROUP = 2
MOBA_LOOKAHEAD = 4
MOBA_KEY_LANES = 128
MOBA_WINDOW = 8
MOBA_SPLIT = 3
MOBA_AUX_ROWS = 8 + MOBA_SPLIT * MOBA_WINDOW
LOG2E = 1.4426950408889634
NORM_EPS = 1e-5
NEG_INF = -1e30

V7X_VMEM_BYTES = 64 * 1024 * 1024
LANES = 128
TOKEN_TILE = 512


def _vmem_limit(estimate_bytes):
    return int(min(V7X_VMEM_BYTES - 4 * 1024 * 1024, max(estimate_bytes, 16 * 1024 * 1024)))


def _nt_dot(a, b):
    return lax.dot_general(a, b, (((1,), (1,)), ((), ())), preferred_element_type=F32)


def _layer_norm(v, g, b):
    mu = jnp.mean(v, axis=-1, keepdims=True)
    d = v - mu
    var = jnp.mean(d * d, axis=-1, keepdims=True)
    return d * lax.rsqrt(var + NORM_EPS) * g + b


def _key_side_pattern(tm):
    hd, sp, win = MOBA_HEAD_DIM, MOBA_SPLIT, MOBA_WINDOW
    assert win % (tm // MOBA_BLOCK) == 0 and 2 * sp <= 8 and hd + MOBA_AUX_ROWS <= MOBA_KEY_LANES
    pos = np.arange(win * MOBA_BLOCK)
    pat = np.zeros((pos.size, MOBA_HEADS, MOBA_KEY_LANES), np.float32)
    pat[:, :, hd:hd + sp] = (pos % MOBA_BLOCK)[:, None, None]
    pat[:, :, hd + sp:hd + 2 * sp] = 1.0
    for s in range(sp):
        for c in range(win):
            pat[:, :, hd + 8 + s * win + c] = (pos // MOBA_BLOCK == c)[:, None]
    return jnp.asarray(pat.reshape(pos.size, MOBA_HEADS * MOBA_KEY_LANES), BF16)


def _inproj_kernel(x_ref, wn_ref, wt_ref, kpat_ref, rqkv_ref, rg_ref, mk_ref, mqt_ref, mvt_ref):
    x = x_ref[0].astype(BF16)
    nat = jnp.dot(x, wn_ref[...], preferred_element_type=F32)
    rqkv_ref[0] = nat[:, :3 * RET_WIDTH].astype(BF16)
    rg_ref[0] = nat[:, 3 * RET_WIDTH:4 * RET_WIDTH]
    mk_ref[0] = nat[:, 4 * RET_WIDTH:].astype(BF16) + kpat_ref[...]
    tr = _nt_dot(wt_ref[...], x)
    for c in range(x.shape[0] // MOBA_BLOCK):
        cols = slice(c * MOBA_BLOCK, (c + 1) * MOBA_BLOCK)
        mqt_ref[0, c] = (tr[:MOBA_WIDTH, cols] * (MOBA_HEAD_DIM ** -0.5 * LOG2E)).astype(BF16)
        mvt_ref[0, c] = tr[MOBA_WIDTH:, cols].astype(BF16)


def _inproj(x, w_nat, w_tr):
    bsz, s_len, d = x.shape
    tm = TOKEN_TILE
    nb = s_len // MOBA_BLOCK
    n_nat = w_nat.shape[1]
    k_width = MOBA_HEADS * MOBA_KEY_LANES
    kpat = _key_side_pattern(tm)
    n_var = kpat.shape[0] // tm
    est = (2 * tm * d * 4 + 2 * (w_nat.size + w_tr.size) * 2 + 2 * tm * k_width * 2
           + 2 * tm * (3 * RET_WIDTH * 2 + RET_WIDTH * 4 + (k_width + 2 * MOBA_WIDTH) * 2)
           + tm * (n_nat + 2 * MOBA_WIDTH) * 4 * 2)
    return pl.pallas_call(
        _inproj_kernel,
        grid=(bsz, s_len // tm),
        in_specs=[
            pl.BlockSpec((1, tm, d), lambda b, t: (b, t, 0)),
            pl.BlockSpec((d, n_nat), lambda b, t: (0, 0)),
            pl.BlockSpec((2 * MOBA_WIDTH, d), lambda b, t: (0, 0)),
            pl.BlockSpec((tm, k_width), lambda b, t: (t % n_var, 0)),
        ],
        out_specs=[
            pl.BlockSpec((1, tm, 3 * RET_WIDTH), lambda b, t: (b, t, 0)),
            pl.BlockSpec((1, tm, RET_WIDTH), lambda b, t: (b, t, 0)),
            pl.BlockSpec((1, tm, k_width), lambda b, t: (b, t, 0)),
            pl.BlockSpec((1, tm // MOBA_BLOCK, MOBA_WIDTH, MOBA_BLOCK), lambda b, t: (b, t, 0, 0)),
            pl.BlockSpec((1, tm // MOBA_BLOCK, MOBA_WIDTH, MOBA_BLOCK), lambda b, t: (b, t, 0, 0)),
        ],
        out_shape=[
            jax.ShapeDtypeStruct((bsz, s_len, 3 * RET_WIDTH), BF16),
            jax.ShapeDtypeStruct((bsz, s_len, RET_WIDTH), F32),
            jax.ShapeDtypeStruct((bsz, s_len, k_width), BF16),
            jax.ShapeDtypeStruct((bsz, nb, MOBA_WIDTH, MOBA_BLOCK), BF16),
            jax.ShapeDtypeStruct((bsz, nb, MOBA_WIDTH, MOBA_BLOCK), BF16),
        ],
        compiler_params=pltpu.CompilerParams(
            dimension_semantics=("parallel", "parallel"),
            vmem_limit_bytes=_vmem_limit(est)),
        name="inproj",
    )(x, w_nat, w_tr, kpat)


def _retention_tables():
    h = np.arange(RET_HEADS, dtype=np.float64)
    log_g = np.log1p(-np.exp2(-5.0 - h))
    pos = np.arange(RET_CHUNK, dtype=np.float64)
    diff = pos[:, None] - pos[None, :]
    scale = RET_HEAD_DIM ** -0.5
    intra = np.where(diff >= 0.0, np.exp(log_g[:, None, None] * np.maximum(diff, 0.0)), 0.0) * scale
    k_to_end = np.exp(log_g[:, None] * (RET_CHUNK - 1.0 - pos)[None, :]) * scale
    q_from_start = np.exp(log_g[:, None] * (pos + 1.0)[None, :])
    chunk_decay = np.exp(log_g * RET_CHUNK)
    bcast = lambda t: np.broadcast_to(t[:, :, None], (RET_HEADS, RET_CHUNK, RET_HEAD_DIM))
    return (jnp.asarray(intra, F32), jnp.asarray(bcast(k_to_end), F32),
            jnp.asarray(bcast(q_from_start), F32), jnp.asarray(chunk_decay, F32))


def _retention_kernel(cd_ref, qkv_ref, rg_ref, gain_ref, dec_ref, kte_ref, qfs_ref, o_ref, state_ref):
    e = RET_HEAD_DIM

    @pl.when(pl.program_id(1) == 0)
    def _():
        state_ref[...] = jnp.zeros_like(state_ref)

    for h in range(RET_HEADS):
        cols = lambda part: slice((part * RET_HEADS + h) * e, (part * RET_HEADS + h + 1) * e)
        q = qkv_ref[0, :, cols(0)]
        k = qkv_ref[0, :, cols(1)]
        v = qkv_ref[0, :, cols(2)]
        scores = _nt_dot(q, k) * dec_ref[h]
        intra = jnp.dot(scores.astype(BF16), v, preferred_element_type=F32)
        prev = state_ref[h]
        cross = jnp.dot(q, prev.astype(BF16), preferred_element_type=F32) * qfs_ref[h]
        y = intra + cross
        k_dec = (k.astype(F32) * kte_ref[h]).T.astype(BF16)
        kv = jnp.dot(k_dec, v, preferred_element_type=F32)
        state_ref[h] = prev * cd_ref[h] + kv
        mu = jnp.mean(y, axis=-1, keepdims=True)
        d = y - mu
        var = jnp.mean(d * d, axis=-1, keepdims=True)
        yn = d * lax.rsqrt(var + NORM_EPS) * gain_ref[:, h * e:(h + 1) * e]
        g = rg_ref[0, :, h * e:(h + 1) * e]
        o_ref[0, :, h * e:(h + 1) * e] = (g * jax.nn.sigmoid(g) * yn).astype(o_ref.dtype)


def _retention(rqkv, rg, gain, tables):
    bsz, s_len, _ = rqkv.shape
    dec, kte, qfs, cd = tables
    c, e = RET_CHUNK, RET_HEAD_DIM
    full = lambda a: pl.BlockSpec(a.shape, lambda b, n: (0,) * a.ndim)
    return pl.pallas_call(
        _retention_kernel,
        grid=(bsz, s_len // c),
        in_specs=[
            pl.BlockSpec(memory_space=pltpu.SMEM),
            pl.BlockSpec((1, c, 3 * RET_WIDTH), lambda b, n: (b, n, 0)),
            pl.BlockSpec((1, c, RET_WIDTH), lambda b, n: (b, n, 0)),
            full(gain), full(dec), full(kte), full(qfs),
        ],
        out_specs=pl.BlockSpec((1, c, RET_WIDTH), lambda b, n: (b, n, 0)),
        out_shape=jax.ShapeDtypeStruct((bsz, s_len, RET_WIDTH), BF16),
        scratch_shapes=[pltpu.VMEM((RET_HEADS, e, e), F32)],
        compiler_params=pltpu.CompilerParams(
            dimension_semantics=("parallel", "arbitrary")),
        name="retention",
    )(cd, rqkv, rg, gain, dec, kte, qfs)


def _alibi_slopes():
    slopes = np.exp2(-8.0 * (np.arange(MOBA_HEADS, dtype=np.float64) + 1.0) / MOBA_HEADS)
    return jnp.asarray(slopes, F32)


def _moba_past_tiles(n_b):
    ceil_div = lambda a: -(-a // MOBA_GROUP)
    totals = {ceil_div(s) + ceil_div(n_b - 1 - s) for s in range(n_b // 2)}
    assert len(totals) == 1 and n_b % 2 == 0
    return totals.pop()


def _moba_kernel(slope_ref, qa_ref, qb_ref, k_ref, vt_ref, oa_ref, ob_ref,
                 kmean_ref, qtop_ref, aux_ref, m_ref, l_ref, acc_ref):
    blk, hd, grp = MOBA_BLOCK, MOBA_HEAD_DIM, MOBA_GROUP
    kl, aux_rows, win, sp = MOBA_KEY_LANES, MOBA_AUX_ROWS, MOBA_WINDOW, MOBA_SPLIT
    tile = grp * blk
    n_b = vt_ref.shape[1]
    n_past = _moba_past_tiles(n_b)
    pair = pl.program_id(1)
    step = pl.program_id(2)
    own = (step, n_b - 1 - step)
    n_past_first = (step + grp - 1) // grp
    q_refs, o_refs = (qa_ref, qb_ref), (oa_ref, ob_ref)

    @pl.when(step == 0)
    def _():
        for b in range(n_b):
            kb = k_ref[0, b * blk:(b + 1) * blk, :].astype(F32)
            kmean_ref[b:b + 1, :] = jnp.sum(kb, axis=0, keepdims=True) * (1.0 / blk)

    jidx = lax.broadcasted_iota(jnp.int32, (n_b, blk), 0)
    causal = (lax.broadcasted_iota(jnp.int32, (blk, blk), 0)
              <= lax.broadcasted_iota(jnp.int32, (blk, blk), 1))
    row8 = lax.broadcasted_iota(jnp.int32, (8, blk), 0)
    r_q8 = lax.broadcasted_iota(jnp.int32, (8, blk), 1).astype(F32)
    zero_rows = lambda n: jnp.zeros((n, blk), BF16)

    def bf16_pieces(v):
        pieces = []
        for _ in range(sp):
            piece = v.astype(BF16).astype(F32)
            pieces.append(piece)
            v = v - piece
        return pieces

    def col_max(u):
        return jnp.max(jnp.max(u.reshape(-1, blk, blk), axis=0), axis=0, keepdims=True)

    def col_sum(p):
        return jnp.sum(jnp.sum(p.reshape(-1, blk, blk), axis=0), axis=0, keepdims=True)

    alibi_rows = []
    for hl in range(2):
        c = slope_ref[2 * pair + hl] * LOG2E
        pieces = bf16_pieces(jnp.where(row8 < sp, c, -c * r_q8))
        rows = jnp.zeros((8, blk), F32)
        for s in range(sp):
            rows = jnp.where((row8 == s) | (row8 == sp + s), pieces[s], rows)
        alibi_rows.append(rows)

    chains = [(o, hl) for o in range(2) for hl in range(2)]
    own_scores, gates = {}, {}
    for o, hl in chains:
        q_top = q_refs[o][0, 0, hl * hd:(hl + 1) * hd, :]
        qtop_ref[o, hl] = q_top
        aux_own = jnp.concatenate([alibi_rows[hl], jnp.zeros((aux_rows - 8, blk), F32)], axis=0)
        q_own = jnp.concatenate([q_top, aux_own.astype(BF16), zero_rows(kl - hd - aux_rows)], axis=0)
        kd = k_ref[0, pl.ds(pl.multiple_of(own[o] * blk, blk), blk), hl * kl:(hl + 1) * kl]
        own_scores[o, hl] = jnp.dot(kd, q_own, preferred_element_type=F32)
        kmean = kmean_ref[:, hl * kl:(hl + 1) * kl].astype(BF16)
        q_gate = jnp.concatenate([q_top, zero_rows(kl - hd)], axis=0)
        gates[o, hl] = jnp.dot(kmean, q_gate, preferred_element_type=F32)

    for o, hl in chains:
        u = jnp.where(causal, own_scores[o, hl], NEG_INF)
        m = col_max(u)
        p = jnp.exp2(u - m)
        m_ref[o, hl] = m
        l_ref[o, hl] = col_sum(p)
        acc_ref[o, hl] = jnp.dot(vt_ref[0, own[o], hl * hd:(hl + 1) * hd, :], p.astype(BF16),
                                 preferred_element_type=F32)

        past = jidx < own[o]
        gate = gates[o, hl]
        taken = jnp.zeros((n_b, blk), jnp.bool_)
        for _ in range(MOBA_TOP_K):
            cand = past & jnp.logical_not(taken)
            best = jnp.max(jnp.where(cand, gate, -jnp.inf), axis=0, keepdims=True)
            pick = cand & (gate == best)
            first = jnp.min(jnp.where(pick, jidx, n_b), axis=0, keepdims=True)
            taken = taken | (jidx == first)
        c = slope_ref[2 * pair + hl] * LOG2E
        block_off = (own[o] - jidx).astype(F32) * (-float(blk) * c)
        bias_pieces = bf16_pieces(jnp.where(taken, block_off, NEG_INF))
        for w in range(n_b // win):
            rows = [alibi_rows[hl]] + [piece[w * win:(w + 1) * win, :] for piece in bias_pieces]
            aux_ref[o, hl, w] = jnp.concatenate(rows, axis=0).astype(BF16)

    tiles = []
    for k in range(n_past):
        which = jnp.where(k >= n_past_first, 1, 0)
        tiles.append((which, k - which * n_past_first))

    def unit_scores(k, hl):
        which, t = tiles[k]
        q_op = jnp.concatenate([qtop_ref[which, hl], aux_ref[which, hl, (t * grp) // win],
                                zero_rows(kl - hd - aux_rows)], axis=0)
        k_rows = k_ref[0, pl.ds(pl.multiple_of(t * tile, tile), tile), hl * kl:(hl + 1) * kl]
        s = jnp.dot(k_rows, q_op, preferred_element_type=F32)
        return s, col_max(s)

    def unit_update(k, hl, s, cmax):
        which, t = tiles[k]
        m = m_ref[which, hl]
        m_new = jnp.maximum(m, cmax)
        alpha = jnp.exp2(m - m_new)
        p = jnp.exp2(s - m_new)
        m_ref[which, hl] = m_new
        l_ref[which, hl] = alpha * l_ref[which, hl] + col_sum(p)
        vt = jnp.concatenate(
            [vt_ref[0, t * grp + c, hl * hd:(hl + 1) * hd, :] for c in range(grp)], axis=1)
        acc_ref[which, hl] = alpha * acc_ref[which, hl] + jnp.dot(
            vt, p.astype(BF16), preferred_element_type=F32)

    units = [(k, hl) for k in range(n_past) for hl in range(2)]
    pending = []
    for idx in range(len(units) + MOBA_LOOKAHEAD):
        if idx < len(units):
            pending.append(unit_scores(*units[idx]))
        if idx >= MOBA_LOOKAHEAD:
            unit_update(*units[idx - MOBA_LOOKAHEAD], *pending.pop(0))

    for o in range(2):
        outs = [acc_ref[o, hl] * (1.0 / l_ref[o, hl]) for hl in range(2)]
        o_refs[o][0] = jnp.concatenate(outs, axis=0).T.astype(oa_ref.dtype)


def _moba(mqt, mk, mvt, slopes):
    bsz, n_b, width, blk = mqt.shape
    s_len = mk.shape[1]
    pw = 2 * MOBA_HEAD_DIM
    kw = 2 * MOBA_KEY_LANES
    tile = MOBA_GROUP * blk
    half = n_b // 2
    est = 2 * s_len * (kw + pw) * 2 + 4 * 2 * 4 * tile * blk * 4
    out_half = jax.ShapeDtypeStruct((bsz, half * blk, width), BF16)
    return pl.pallas_call(
        _moba_kernel,
        grid=(bsz, width // pw, half),
        in_specs=[
            pl.BlockSpec(memory_space=pltpu.SMEM),
            pl.BlockSpec((1, 1, pw, blk), lambda b, p, s: (b, s, p, 0)),
            pl.BlockSpec((1, 1, pw, blk), lambda b, p, s: (b, n_b - 1 - s, p, 0)),
            pl.BlockSpec((1, s_len, kw), lambda b, p, s: (b, 0, p)),
            pl.BlockSpec((1, n_b, pw, blk), lambda b, p, s: (b, 0, p, 0)),
        ],
        out_specs=[pl.BlockSpec((1, blk, pw), lambda b, p, s: (b, s, p)),
                   pl.BlockSpec((1, blk, pw), lambda b, p, s: (b, half - 1 - s, p))],
        out_shape=[out_half, out_half],
        scratch_shapes=[pltpu.VMEM((n_b, kw), F32),
                        pltpu.VMEM((2, 2, MOBA_HEAD_DIM, blk), BF16),
                        pltpu.VMEM((2, 2, n_b // MOBA_WINDOW, MOBA_AUX_ROWS, blk), BF16),
                        pltpu.VMEM((2, 2, 1, blk), F32),
                        pltpu.VMEM((2, 2, 1, blk), F32),
                        pltpu.VMEM((2, 2, MOBA_HEAD_DIM, blk), F32)],
        compiler_params=pltpu.CompilerParams(
            dimension_semantics=("parallel", "parallel", "arbitrary"),
            vmem_limit_bytes=_vmem_limit(est)),
        name="moba",
    )(slopes, mqt, mqt, mk, mvt)


def _mix_ffn_kernel(alpha, lo_tiles, x_ref, ret_ref, mlo_ref, mhi_ref, wr_ref, wm_ref, g1_ref, b1_ref,
                    wg_ref, wu_ref, wd_ref, g2_ref, b2_ref, o_ref):
    moba = jnp.where(pl.program_id(1) < lo_tiles, mlo_ref[0], mhi_ref[0])
    mixed = (jnp.dot(ret_ref[0], wr_ref[...], preferred_element_type=F32)
             + jnp.dot(moba, wm_ref[...], preferred_element_type=F32))
    h = _layer_norm(alpha * x_ref[0] + mixed, g1_ref[...], b1_ref[...])
    hb = h.astype(BF16)
    gate = jnp.dot(hb, wg_ref[...], preferred_element_type=F32)
    up = jnp.dot(hb, wu_ref[...], preferred_element_type=F32)
    act = (gate * jax.nn.sigmoid(gate) * up).astype(BF16)
    ffn = jnp.dot(act, wd_ref[...], preferred_element_type=F32)
    o_ref[0] = _layer_norm(alpha * h + ffn, g2_ref[...], b2_ref[...])


def _mix_ffn(x, ret, moba_lo, moba_hi, w_ret, w_moba, g1, b1, w_gate, w_up, w_down, g2, b2, alpha):
    bsz, s_len, d = x.shape
    d_ff = w_gate.shape[1]
    tm = TOKEN_TILE
    lo_tiles = moba_lo.shape[1] // tm
    assert moba_lo.shape[1] % tm == 0 and moba_hi.shape[1] == s_len - moba_lo.shape[1]
    row = lambda w: pl.BlockSpec((1, tm, w), lambda b_, t: (b_, t, 0))
    lo_row = pl.BlockSpec((1, tm, MOBA_WIDTH), lambda b_, t: (b_, jnp.minimum(t, lo_tiles - 1), 0))
    hi_row = pl.BlockSpec((1, tm, MOBA_WIDTH), lambda b_, t: (b_, jnp.maximum(t - lo_tiles, 0), 0))
    full = lambda a: pl.BlockSpec(a.shape, lambda b_, t: (0, 0), pipeline_mode=pl.Buffered(1))
    est = ((3 * d * d_ff + d * d) * 2 + 2 * 2 * tm * d * 4 + 2 * 3 * tm * RET_WIDTH * 2
           + tm * d_ff * (4 + 4 + 2) + 3 * tm * d * 4)
    return pl.pallas_call(
        functools.partial(_mix_ffn_kernel, alpha, lo_tiles),
        grid=(bsz, s_len // tm),
        in_specs=[row(d), row(RET_WIDTH), lo_row, hi_row, full(w_ret), full(w_moba), full(g1), full(b1),
                  full(w_gate), full(w_up), full(w_down), full(g2), full(b2)],
        out_specs=row(d),
        out_shape=jax.ShapeDtypeStruct((bsz, s_len, d), F32),
        compiler_params=pltpu.CompilerParams(
            dimension_semantics=("parallel", "parallel"),
            vmem_limit_bytes=_vmem_limit(est)),
        name="mix_ffn",
    )(x, ret, moba_lo, moba_hi, w_ret, w_moba, g1, b1, w_gate, w_up, w_down, g2, b2)


def kernel(x, w_in, ret_gn_gain, w_out, ln1_g, ln1_b, w_gate, w_up, w_down, ln2_g, ln2_b):
    bsz, s_len, d = x.shape
    depth = w_in.shape[0]
    assert s_len % math.lcm(RET_CHUNK, MOBA_BLOCK, TOKEN_TILE) == 0
    assert w_in.shape[2] == 4 * RET_WIDTH + 3 * MOBA_WIDTH
    alpha = (2 * depth) ** 0.25
    ret_tables = _retention_tables()
    slopes = _alibi_slopes()
    r4 = 4 * RET_WIDTH
    h = x
    for layer in range(depth):
        w = w_in[layer]
        w_mk = w[:, r4 + MOBA_WIDTH:r4 + 2 * MOBA_WIDTH].reshape(d, MOBA_HEADS, MOBA_HEAD_DIM)
        w_mk = jnp.pad(w_mk, ((0, 0), (0, 0), (0, MOBA_KEY_LANES - MOBA_HEAD_DIM)))
        w_nat = jnp.concatenate([w[:, :r4], w_mk.reshape(d, MOBA_HEADS * MOBA_KEY_LANES)], axis=1).astype(BF16)
        w_tr = jnp.concatenate([w[:, r4:r4 + MOBA_WIDTH], w[:, r4 + 2 * MOBA_WIDTH:]], axis=1).T.astype(BF16)
        rqkv, rg, mk, mqt, mvt = _inproj(h, w_nat, w_tr)
        ret = _retention(rqkv, rg, ret_gn_gain[layer][None, :], ret_tables)
        moba_lo, moba_hi = _moba(mqt, mk, mvt, slopes)
        wo = w_out[layer].astype(BF16)
        h = _mix_ffn(h, ret, moba_lo, moba_hi, wo[:RET_WIDTH], wo[RET_WIDTH:],
                     ln1_g[layer][None, :], ln1_b[layer][None, :],
                     w_gate[layer].astype(BF16), w_up[layer].astype(BF16), w_down[layer].astype(BF16),
                     ln2_g[layer][None, :], ln2_b[layer][None, :], alpha)
    return h
```

```python
import functools
import math

import numpy as np
import jax
import jax.numpy as jnp
from jax import lax
from jax.experimental import pallas as pl
from jax.experimental.pallas import tpu as pltpu

F32 = jnp.float32
BF16 = jnp.bfloat16

RET_HEADS = 4
RET_HEAD_DIM = 128
RET_WIDTH = RET_HEADS * RET_HEAD_DIM
RET_CHUNK = 256
MOBA_HEADS = 8
MOBA_HEAD_DIM = 64
MOBA_WIDTH = MOBA_HEADS * MOBA_HEAD_DIM
MOBA_BLOCK = 256
MOBA_TOP_K = 3
MOBA_GROUP = 2
MOBA_LOOKAHEAD = 4
MOBA_SUM_ROWS = 16
MOBA_KEY_LANES = 128
MOBA_WINDOW = 8
MOBA_SPLIT = 3
MOBA_AUX_ROWS = 8 + MOBA_SPLIT * MOBA_WINDOW
LOG2E = 1.4426950408889634
NORM_EPS = 1e-5
NEG_INF = -1e30

V7X_VMEM_BYTES = 64 * 1024 * 1024
LANES = 128
TOKEN_TILE = 512


def _vmem_limit(estimate_bytes):
    return int(min(V7X_VMEM_BYTES - 4 * 1024 * 1024, max(estimate_bytes, 16 * 1024 * 1024)))


def _nt_dot(a, b):
    return lax.dot_general(a, b, (((1,), (1,)), ((), ())), preferred_element_type=F32)


def _layer_norm(v, g, b):
    mu = jnp.mean(v, axis=-1, keepdims=True)
    d = v - mu
    var = jnp.mean(d * d, axis=-1, keepdims=True)
    return d * lax.rsqrt(var + NORM_EPS) * g + b


def _key_side_pattern(tm):
    hd, sp, win = MOBA_HEAD_DIM, MOBA_SPLIT, MOBA_WINDOW
    assert win % (tm // MOBA_BLOCK) == 0 and 2 * sp <= 8 and hd + MOBA_AUX_ROWS <= MOBA_KEY_LANES
    pos = np.arange(win * MOBA_BLOCK)
    pat = np.zeros((pos.size, MOBA_HEADS, MOBA_KEY_LANES), np.float32)
    pat[:, :, hd:hd + sp] = (pos % MOBA_BLOCK)[:, None, None]
    pat[:, :, hd + sp:hd + 2 * sp] = 1.0
    for s in range(sp):
        for c in range(win):
            pat[:, :, hd + 8 + s * win + c] = (pos // MOBA_BLOCK == c)[:, None]
    return jnp.asarray(pat.reshape(pos.size, MOBA_HEADS * MOBA_KEY_LANES), BF16)


def _inproj_kernel(x_ref, wn_ref, wt_ref, kpat_ref, rqkv_ref, rg_ref, mk_ref, mqt_ref, mvt_ref):
    x = x_ref[0].astype(BF16)
    nat = jnp.dot(x, wn_ref[...], preferred_element_type=F32)
    rqkv_ref[0] = nat[:, :3 * RET_WIDTH].astype(BF16)
    rg_ref[0] = nat[:, 3 * RET_WIDTH:4 * RET_WIDTH]
    mk_ref[0] = nat[:, 4 * RET_WIDTH:].astype(BF16) + kpat_ref[...]
    tr = _nt_dot(wt_ref[...], x)
    for c in range(x.shape[0] // MOBA_BLOCK):
        cols = slice(c * MOBA_BLOCK, (c + 1) * MOBA_BLOCK)
        mqt_ref[0, c] = (tr[:MOBA_WIDTH, cols] * (MOBA_HEAD_DIM ** -0.5 * LOG2E)).astype(BF16)
        mvt_ref[0, c] = tr[MOBA_WIDTH:, cols].astype(BF16)


def _inproj(x, w_nat, w_tr):
    bsz, s_len, d = x.shape
    tm = TOKEN_TILE
    nb = s_len // MOBA_BLOCK
    n_nat = w_nat.shape[1]
    k_width = MOBA_HEADS * MOBA_KEY_LANES
    kpat = _key_side_pattern(tm)
    n_var = kpat.shape[0] // tm
    est = (2 * tm * d * 4 + 2 * (w_nat.size + w_tr.size) * 2 + 2 * tm * k_width * 2
           + 2 * tm * (3 * RET_WIDTH * 2 + RET_WIDTH * 4 + (k_width + 2 * MOBA_WIDTH) * 2)
           + tm * (n_nat + 2 * MOBA_WIDTH) * 4 * 2)
    return pl.pallas_call(
        _inproj_kernel,
        grid=(bsz, s_len // tm),
        in_specs=[
            pl.BlockSpec((1, tm, d), lambda b, t: (b, t, 0)),
            pl.BlockSpec((d, n_nat), lambda b, t: (0, 0)),
            pl.BlockSpec((2 * MOBA_WIDTH, d), lambda b, t: (0, 0)),
            pl.BlockSpec((tm, k_width), lambda b, t: (t % n_var, 0)),
        ],
        out_specs=[
            pl.BlockSpec((1, tm, 3 * RET_WIDTH), lambda b, t: (b, t, 0)),
            pl.BlockSpec((1, tm, RET_WIDTH), lambda b, t: (b, t, 0)),
            pl.BlockSpec((1, tm, k_width), lambda b, t: (b, t, 0)),
            pl.BlockSpec((1, tm // MOBA_BLOCK, MOBA_WIDTH, MOBA_BLOCK), lambda b, t: (b, t, 0, 0)),
            pl.BlockSpec((1, tm // MOBA_BLOCK, MOBA_WIDTH, MOBA_BLOCK), lambda b, t: (b, t, 0, 0)),
        ],
        out_shape=[
            jax.ShapeDtypeStruct((bsz, s_len, 3 * RET_WIDTH), BF16),
            jax.ShapeDtypeStruct((bsz, s_len, RET_WIDTH), F32),
            jax.ShapeDtypeStruct((bsz, s_len, k_width), BF16),
            jax.ShapeDtypeStruct((bsz, nb, MOBA_WIDTH, MOBA_BLOCK), BF16),
            jax.ShapeDtypeStruct((bsz, nb, MOBA_WIDTH, MOBA_BLOCK), BF16),
        ],
        compiler_params=pltpu.CompilerParams(
            dimension_semantics=("parallel", "parallel"),
            vmem_limit_bytes=_vmem_limit(est)),
        name="inproj",
    )(x, w_nat, w_tr, kpat)


def _retention_tables():
    h = np.arange(RET_HEADS, dtype=np.float64)
    log_g = np.log1p(-np.exp2(-5.0 - h))
    pos = np.arange(RET_CHUNK, dtype=np.float64)
    diff = pos[:, None] - pos[None, :]
    scale = RET_HEAD_DIM ** -0.5
    intra = np.where(diff >= 0.0, np.exp(log_g[:, None, None] * np.maximum(diff, 0.0)), 0.0) * scale
    k_to_end = np.exp(log_g[:, None] * (RET_CHUNK - 1.0 - pos)[None, :]) * scale
    q_from_start = np.exp(log_g[:, None] * (pos + 1.0)[None, :])
    chunk_decay = np.exp(log_g * RET_CHUNK)
    bcast = lambda t: np.broadcast_to(t[:, :, None], (RET_HEADS, RET_CHUNK, RET_HEAD_DIM))
    return (jnp.asarray(intra, F32), jnp.asarray(bcast(k_to_end), F32),
            jnp.asarray(bcast(q_from_start), F32), jnp.asarray(chunk_decay, F32))


def _retention_kernel(cd_ref, qkv_ref, rg_ref, gain_ref, dec_ref, kte_ref, qfs_ref, o_ref, state_ref):
    e = RET_HEAD_DIM

    @pl.when(pl.program_id(1) == 0)
    def _():
        state_ref[...] = jnp.zeros_like(state_ref)

    for h in range(RET_HEADS):
        cols = lambda part: slice((part * RET_HEADS + h) * e, (part * RET_HEADS + h + 1) * e)
        q = qkv_ref[0, :, cols(0)]
        k = qkv_ref[0, :, cols(1)]
        v = qkv_ref[0, :, cols(2)]
        scores = _nt_dot(q, k) * dec_ref[h]
        intra = jnp.dot(scores.astype(BF16), v, preferred_element_type=F32)
        prev = state_ref[h]
        cross = jnp.dot(q, prev.astype(BF16), preferred_element_type=F32) * qfs_ref[h]
        y = intra + cross
        k_dec = (k.astype(F32) * kte_ref[h]).T.astype(BF16)
        kv = jnp.dot(k_dec, v, preferred_element_type=F32)
        state_ref[h] = prev * cd_ref[h] + kv
        mu = jnp.mean(y, axis=-1, keepdims=True)
        d = y - mu
        var = jnp.mean(d * d, axis=-1, keepdims=True)
        yn = d * lax.rsqrt(var + NORM_EPS) * gain_ref[:, h * e:(h + 1) * e]
        g = rg_ref[0, :, h * e:(h + 1) * e]
        o_ref[0, :, h * e:(h + 1) * e] = (g * jax.nn.sigmoid(g) * yn).astype(o_ref.dtype)


def _retention(rqkv, rg, gain, tables):
    bsz, s_len, _ = rqkv.shape
    dec, kte, qfs, cd = tables
    c, e = RET_CHUNK, RET_HEAD_DIM
    full = lambda a: pl.BlockSpec(a.shape, lambda b, n: (0,) * a.ndim)
    return pl.pallas_call(
        _retention_kernel,
        grid=(bsz, s_len // c),
        in_specs=[
            pl.BlockSpec(memory_space=pltpu.SMEM),
            pl.BlockSpec((1, c, 3 * RET_WIDTH), lambda b, n: (b, n, 0)),
            pl.BlockSpec((1, c, RET_WIDTH), lambda b, n: (b, n, 0)),
            full(gain), full(dec), full(kte), full(qfs),
        ],
        out_specs=pl.BlockSpec((1, c, RET_WIDTH), lambda b, n: (b, n, 0)),
        out_shape=jax.ShapeDtypeStruct((bsz, s_len, RET_WIDTH), BF16),
        scratch_shapes=[pltpu.VMEM((RET_HEADS, e, e), F32)],
        compiler_params=pltpu.CompilerParams(
            dimension_semantics=("parallel", "arbitrary")),
        name="retention",
    )(cd, rqkv, rg, gain, dec, kte, qfs)


def _alibi_slopes():
    slopes = np.exp2(-8.0 * (np.arange(MOBA_HEADS, dtype=np.float64) + 1.0) / MOBA_HEADS)
    return jnp.asarray(slopes, F32)


def _moba_past_tiles(n_b):
    ceil_div = lambda a: -(-a // MOBA_GROUP)
    totals = {ceil_div(s) + ceil_div(n_b - 1 - s) for s in range(n_b // 2)}
    assert len(totals) == 1 and n_b % 2 == 0
    return totals.pop()


def _moba_kernel(slope_ref, qa_ref, qb_ref, k_ref, vt_ref, oa_ref, ob_ref,
                 kmean_ref, qtop_ref, aux_ref, m_ref, acc_ref):
    blk, hd, grp = MOBA_BLOCK, MOBA_HEAD_DIM, MOBA_GROUP
    kl, aux_rows, win, sp = MOBA_KEY_LANES, MOBA_AUX_ROWS, MOBA_WINDOW, MOBA_SPLIT
    tile = grp * blk
    n_b = vt_ref.shape[1]
    n_past = _moba_past_tiles(n_b)
    pair = pl.program_id(1)
    step = pl.program_id(2)
    own = (step, n_b - 1 - step)
    n_past_first = (step + grp - 1) // grp
    q_refs, o_refs = (qa_ref, qb_ref), (oa_ref, ob_ref)

    @pl.when(step == 0)
    def _():
        for b in range(n_b):
            kb = k_ref[0, b * blk:(b + 1) * blk, :].astype(F32)
            kmean_ref[b:b + 1, :] = jnp.sum(kb, axis=0, keepdims=True) * (1.0 / blk)

    jidx = lax.broadcasted_iota(jnp.int32, (n_b, blk), 0)
    causal = (lax.broadcasted_iota(jnp.int32, (blk, blk), 0)
              <= lax.broadcasted_iota(jnp.int32, (blk, blk), 1))
    row8 = lax.broadcasted_iota(jnp.int32, (8, blk), 0)
    r_q8 = lax.broadcasted_iota(jnp.int32, (8, blk), 1).astype(F32)
    zero_rows = lambda n: jnp.zeros((n, blk), BF16)
    ones_rows = lambda n: (lax.broadcasted_iota(jnp.int32, (MOBA_SUM_ROWS, n), 0) == 0).astype(BF16)

    def bf16_pieces(v):
        pieces = []
        for _ in range(sp):
            piece = v.astype(BF16).astype(F32)
            pieces.append(piece)
            v = v - piece
        return pieces

    def col_max(u):
        return jnp.max(jnp.max(u.reshape(-1, blk, blk), axis=0), axis=0, keepdims=True)

    alibi_rows = []
    for hl in range(2):
        c = slope_ref[2 * pair + hl] * LOG2E
        pieces = bf16_pieces(jnp.where(row8 < sp, c, -c * r_q8))
        rows = jnp.zeros((8, blk), F32)
        for s in range(sp):
            rows = jnp.where((row8 == s) | (row8 == sp + s), pieces[s], rows)
        alibi_rows.append(rows)

    chains = [(o, hl) for o in range(2) for hl in range(2)]
    own_scores, gates = {}, {}
    for o, hl in chains:
        q_top = q_refs[o][0, 0, hl * hd:(hl + 1) * hd, :]
        qtop_ref[o, hl] = q_top
        aux_own = jnp.concatenate([alibi_rows[hl], jnp.zeros((aux_rows - 8, blk), F32)], axis=0)
        q_own = jnp.concatenate([q_top, aux_own.astype(BF16), zero_rows(kl - hd - aux_rows)], axis=0)
        kd = k_ref[0, pl.ds(pl.multiple_of(own[o] * blk, blk), blk), hl * kl:(hl + 1) * kl]
        own_scores[o, hl] = jnp.dot(kd, q_own, preferred_element_type=F32)
        kmean = kmean_ref[:, hl * kl:(hl + 1) * kl].astype(BF16)
        q_gate = jnp.concatenate([q_top, zero_rows(kl - hd)], axis=0)
        gates[o, hl] = jnp.dot(kmean, q_gate, preferred_element_type=F32)

    for o, hl in chains:
        u = jnp.where(causal, own_scores[o, hl], NEG_INF)
        m = col_max(u)
        p = jnp.exp2(u - m)
        m_ref[o, hl] = m
        vt = jnp.concatenate([vt_ref[0, own[o], hl * hd:(hl + 1) * hd, :], ones_rows(blk)], axis=0)
        acc_ref[o, hl] = jnp.dot(vt, p.astype(BF16), preferred_element_type=F32)

        past = jidx < own[o]
        gate = gates[o, hl]
        taken = jnp.zeros((n_b, blk), jnp.bool_)
        for _ in range(MOBA_TOP_K):
            cand = past & jnp.logical_not(taken)
            best = jnp.max(jnp.where(cand, gate, -jnp.inf), axis=0, keepdims=True)
            pick = cand & (gate == best)
            first = jnp.min(jnp.where(pick, jidx, n_b), axis=0, keepdims=True)
            taken = taken | (jidx == first)
        c = slope_ref[2 * pair + hl] * LOG2E
        block_off = (own[o] - jidx).astype(F32) * (-float(blk) * c)
        bias_pieces = bf16_pieces(jnp.where(taken, block_off, NEG_INF))
        for w in range(n_b // win):
            rows = [alibi_rows[hl]] + [piece[w * win:(w + 1) * win, :] for piece in bias_pieces]
            aux_ref[o, hl, w] = jnp.concatenate(rows, axis=0).astype(BF16)

    tiles = []
    for k in range(n_past):
        which = jnp.where(k >= n_past_first, 1, 0)
        tiles.append((which, k - which * n_past_first))

    def unit_scores(k, hl):
        which, t = tiles[k]
        q_op = jnp.concatenate([qtop_ref[which, hl], aux_ref[which, hl, (t * grp) // win],
                                zero_rows(kl - hd - aux_rows)], axis=0)
        k_rows = k_ref[0, pl.ds(pl.multiple_of(t * tile, tile), tile), hl * kl:(hl + 1) * kl]
        s = jnp.dot(k_rows, q_op, preferred_element_type=F32)
        return s, col_max(s)

    def unit_update(k, hl, s, cmax):
        which, t = tiles[k]
        m = m_ref[which, hl]
        m_new = jnp.maximum(m, cmax)
        alpha = jnp.exp2(m - m_new)
        p = jnp.exp2(s - m_new)
        m_ref[which, hl] = m_new
        vt = jnp.concatenate(
            [vt_ref[0, t * grp + c, hl * hd:(hl + 1) * hd, :] for c in range(grp)], axis=1)
        vt = jnp.concatenate([vt, ones_rows(tile)], axis=0)
        acc_ref[which, hl] = alpha * acc_ref[which, hl] + jnp.dot(
            vt, p.astype(BF16), preferred_element_type=F32)

    units = [(k, hl) for k in range(n_past) for hl in range(2)]
    pending = []
    for idx in range(len(units) + MOBA_LOOKAHEAD):
        if idx < len(units):
            pending.append(unit_scores(*units[idx]))
        if idx >= MOBA_LOOKAHEAD:
            unit_update(*units[idx - MOBA_LOOKAHEAD], *pending.pop(0))

    for o in range(2):
        outs = [acc_ref[o, hl, :hd, :] * (1.0 / acc_ref[o, hl, hd:hd + 1, :]) for hl in range(2)]
        o_refs[o][0] = jnp.concatenate(outs, axis=0).T.astype(oa_ref.dtype)


def _moba(mqt, mk, mvt, slopes):
    bsz, n_b, width, blk = mqt.shape
    s_len = mk.shape[1]
    pw = 2 * MOBA_HEAD_DIM
    kw = 2 * MOBA_KEY_LANES
    tile = MOBA_GROUP * blk
    half = n_b // 2
    est = 2 * s_len * (kw + pw) * 2 + 4 * 2 * 4 * tile * blk * 4
    out_half = jax.ShapeDtypeStruct((bsz, half * blk, width), BF16)
    return pl.pallas_call(
        _moba_kernel,
        grid=(bsz, width // pw, half),
        in_specs=[
            pl.BlockSpec(memory_space=pltpu.SMEM),
            pl.BlockSpec((1, 1, pw, blk), lambda b, p, s: (b, s, p, 0)),
            pl.BlockSpec((1, 1, pw, blk), lambda b, p, s: (b, n_b - 1 - s, p, 0)),
            pl.BlockSpec((1, s_len, kw), lambda b, p, s: (b, 0, p)),
            pl.BlockSpec((1, n_b, pw, blk), lambda b, p, s: (b, 0, p, 0)),
        ],
        out_specs=[pl.BlockSpec((1, blk, pw), lambda b, p, s: (b, s, p)),
                   pl.BlockSpec((1, blk, pw), lambda b, p, s: (b, half - 1 - s, p))],
        out_shape=[out_half, out_half],
        scratch_shapes=[pltpu.VMEM((n_b, kw), F32),
                        pltpu.VMEM((2, 2, MOBA_HEAD_DIM, blk), BF16),
                        pltpu.VMEM((2, 2, n_b // MOBA_WINDOW, MOBA_AUX_ROWS, blk), BF16),
                        pltpu.VMEM((2, 2, 1, blk), F32),
                        pltpu.VMEM((2, 2, MOBA_HEAD_DIM + MOBA_SUM_ROWS, blk), F32)],
        compiler_params=pltpu.CompilerParams(
            dimension_semantics=("parallel", "parallel", "arbitrary"),
            vmem_limit_bytes=_vmem_limit(est)),
        name="moba",
    )(slopes, mqt, mqt, mk, mvt)


def _mix_ffn_kernel(alpha, lo_tiles, x_ref, ret_ref, mlo_ref, mhi_ref, wr_ref, wm_ref, g1_ref, b1_ref,
                    wg_ref, wu_ref, wd_ref, g2_ref, b2_ref, o_ref):
    moba = jnp.where(pl.program_id(1) < lo_tiles, mlo_ref[0], mhi_ref[0])
    mixed = (jnp.dot(ret_ref[0], wr_ref[...], preferred_element_type=F32)
             + jnp.dot(moba, wm_ref[...], preferred_element_type=F32))
    h = _layer_norm(alpha * x_ref[0] + mixed, g1_ref[...], b1_ref[...])
    hb = h.astype(BF16)
    gate = jnp.dot(hb, wg_ref[...], preferred_element_type=F32)
    up = jnp.dot(hb, wu_ref[...], preferred_element_type=F32)
    act = (gate * jax.nn.sigmoid(gate) * up).astype(BF16)
    ffn = jnp.dot(act, wd_ref[...], preferred_element_type=F32)
    o_ref[0] = _layer_norm(alpha * h + ffn, g2_ref[...], b2_ref[...])


def _mix_ffn(x, ret, moba_lo, moba_hi, w_ret, w_moba, g1, b1, w_gate, w_up, w_down, g2, b2, alpha):
    bsz, s_len, d = x.shape
    d_ff = w_gate.shape[1]
    tm = TOKEN_TILE
    lo_tiles = moba_lo.shape[1] // tm
    assert moba_lo.shape[1] % tm == 0 and moba_hi.shape[1] == s_len - moba_lo.shape[1]
    row = lambda w: pl.BlockSpec((1, tm, w), lambda b_, t: (b_, t, 0))
    lo_row = pl.BlockSpec((1, tm, MOBA_WIDTH), lambda b_, t: (b_, jnp.minimum(t, lo_tiles - 1), 0))
    hi_row = pl.BlockSpec((1, tm, MOBA_WIDTH), lambda b_, t: (b_, jnp.maximum(t - lo_tiles, 0), 0))
    full = lambda a: pl.BlockSpec(a.shape, lambda b_, t: (0, 0), pipeline_mode=pl.Buffered(1))
    est = ((3 * d * d_ff + d * d) * 2 + 2 * 2 * tm * d * 4 + 2 * 3 * tm * RET_WIDTH * 2
           + tm * d_ff * (4 + 4 + 2) + 3 * tm * d * 4)
    return pl.pallas_call(
        functools.partial(_mix_ffn_kernel, alpha, lo_tiles),
        grid=(bsz, s_len // tm),
        in_specs=[row(d), row(RET_WIDTH), lo_row, hi_row, full(w_ret), full(w_moba), full(g1), full(b1),
                  full(w_gate), full(w_up), full(w_down), full(g2), full(b2)],
        out_specs=row(d),
        out_shape=jax.ShapeDtypeStruct((bsz, s_len, d), F32),
        compiler_params=pltpu.CompilerParams(
            dimension_semantics=("parallel", "parallel"),
            vmem_limit_bytes=_vmem_limit(est)),
        name="mix_ffn",
    )(x, ret, moba_lo, moba_hi, w_ret, w_moba, g1, b1, w_gate, w_up, w_down, g2, b2)


def kernel(x, w_in, ret_gn_gain, w_out, ln1_g, ln1_b, w_gate, w_up, w_down, ln2_g, ln2_b):
    bsz, s_len, d = x.shape
    depth = w_in.shape[0]
    assert s_len % math.lcm(RET_CHUNK, MOBA_BLOCK, TOKEN_TILE) == 0
    assert w_in.shape[2] == 4 * RET_WIDTH + 3 * MOBA_WIDTH
    alpha = (2 * depth) ** 0.25
    ret_tables = _retention_tables()
    slopes = _alibi_slopes()
    r4 = 4 * RET_WIDTH
    h = x
    for layer in range(depth):
        w = w_in[layer]
        w_mk = w[:, r4 + MOBA_WIDTH:r4 + 2 * MOBA_WIDTH].reshape(d, MOBA_HEADS, MOBA_HEAD_DIM)
        w_mk = jnp.pad(w_mk, ((0, 0), (0, 0), (0, MOBA_KEY_LANES - MOBA_HEAD_DIM)))
        w_nat = jnp.concatenate([w[:, :r4], w_mk.reshape(d, MOBA_HEADS * MOBA_KEY_LANES)], axis=1).astype(BF16)
        w_tr = jnp.concatenate([w[:, r4:r4 + MOBA_WIDTH], w[:, r4 + 2 * MOBA_WIDTH:]], axis=1).T.astype(BF16)
        rqkv, rg, mk, mqt, mvt = _inproj(h, w_nat, w_tr)
        ret = _retention(rqkv, rg, ret_gn_gain[layer][None, :], ret_tables)
        moba_lo, moba_hi = _moba(mqt, mk, mvt, slopes)
        wo = w_out[layer].astype(BF16)
        h = _mix_ffn(h, ret, moba_lo, moba_hi, wo[:RET_WIDTH], wo[RET_WIDTH:],
                     ln1_g[layer][None, :], ln1_b[layer][None, :],
                     w_gate[layer].astype(BF16), w_up[layer].astype(BF16), w_down[layer].astype(BF16),
                     ln2_g[layer][None, :], ln2_b[layer][None, :], alpha)
    return h
```

```python
import functools
import math

import numpy as np
import jax
import jax.numpy as jnp
from jax import lax
from jax.experimental import pallas as pl
from jax.experimental.pallas import tpu as pltpu

F32 = jnp.float32
BF16 = jnp.bfloat16

RET_HEADS = 4
RET_HEAD_DIM = 128
RET_WIDTH = RET_HEADS * RET_HEAD_DIM
RET_CHUNK = 256
RET_CHUNKS_PER_STEP = 4
MOBA_HEADS = 8
MOBA_HEAD_DIM = 64
MOBA_WIDTH = MOBA_HEADS * MOBA_HEAD_DIM
MOBA_BLOCK = 256
MOBA_TOP_K = 3
MOBA_GROUP = 2
MOBA_LOOKAHEAD = 4
MOBA_SUM_ROWS = 16
MOBA_KEY_LANES = 128
MOBA_WINDOW = 8
MOBA_SPLIT = 3
MOBA_AUX_ROWS = 8 + MOBA_SPLIT * MOBA_WINDOW
LOG2E = 1.4426950408889634
NORM_EPS = 1e-5
NEG_INF = -1e30

V7X_VMEM_BYTES = 64 * 1024 * 1024
LANES = 128
TOKEN_TILE = 512
FFN_SUBTILES = 2


def _vmem_limit(estimate_bytes):
    return int(min(V7X_VMEM_BYTES - 4 * 1024 * 1024, max(estimate_bytes, 16 * 1024 * 1024)))


def _nt_dot(a, b):
    return lax.dot_general(a, b, (((1,), (1,)), ((), ())), preferred_element_type=F32)


def _layer_norm(v, g, b):
    mu = jnp.mean(v, axis=-1, keepdims=True)
    d = v - mu
    var = jnp.mean(d * d, axis=-1, keepdims=True)
    return d * lax.rsqrt(var + NORM_EPS) * g + b


def _key_side_pattern(tm):
    hd, sp, win = MOBA_HEAD_DIM, MOBA_SPLIT, MOBA_WINDOW
    assert win % (tm // MOBA_BLOCK) == 0 and 2 * sp <= 8 and hd + MOBA_AUX_ROWS <= MOBA_KEY_LANES
    pos = np.arange(win * MOBA_BLOCK)
    pat = np.zeros((pos.size, MOBA_HEADS, MOBA_KEY_LANES), np.float32)
    pat[:, :, hd:hd + sp] = (pos % MOBA_BLOCK)[:, None, None]
    pat[:, :, hd + sp:hd + 2 * sp] = 1.0
    for s in range(sp):
        for c in range(win):
            pat[:, :, hd + 8 + s * win + c] = (pos // MOBA_BLOCK == c)[:, None]
    return jnp.asarray(pat.reshape(pos.size, MOBA_HEADS * MOBA_KEY_LANES), BF16)


def _inproj_kernel(x_ref, wn_ref, wt_ref, kpat_ref, rqkv_ref, rg_ref, mk_ref, mqt_ref, mvt_ref):
    x = x_ref[0].astype(BF16)
    nat = jnp.dot(x, wn_ref[...], preferred_element_type=F32)
    rqkv_ref[0] = nat[:, :3 * RET_WIDTH].astype(BF16)
    rg_ref[0] = nat[:, 3 * RET_WIDTH:4 * RET_WIDTH]
    mk_ref[0] = nat[:, 4 * RET_WIDTH:].astype(BF16) + kpat_ref[...]
    tr = _nt_dot(wt_ref[...], x)
    for c in range(x.shape[0] // MOBA_BLOCK):
        cols = slice(c * MOBA_BLOCK, (c + 1) * MOBA_BLOCK)
        mqt_ref[0, c] = (tr[:MOBA_WIDTH, cols] * (MOBA_HEAD_DIM ** -0.5 * LOG2E)).astype(BF16)
        mvt_ref[0, c] = tr[MOBA_WIDTH:, cols].astype(BF16)


def _inproj(x, w_nat, w_tr):
    bsz, s_len, d = x.shape
    tm = TOKEN_TILE
    nb = s_len // MOBA_BLOCK
    n_nat = w_nat.shape[1]
    k_width = MOBA_HEADS * MOBA_KEY_LANES
    kpat = _key_side_pattern(tm)
    n_var = kpat.shape[0] // tm
    est = (2 * tm * d * 4 + 2 * (w_nat.size + w_tr.size) * 2 + 2 * tm * k_width * 2
           + 2 * tm * (3 * RET_WIDTH * 2 + RET_WIDTH * 4 + (k_width + 2 * MOBA_WIDTH) * 2)
           + tm * (n_nat + 2 * MOBA_WIDTH) * 4 * 2)
    return pl.pallas_call(
        _inproj_kernel,
        grid=(bsz, s_len // tm),
        in_specs=[
            pl.BlockSpec((1, tm, d), lambda b, t: (b, t, 0)),
            pl.BlockSpec((d, n_nat), lambda b, t: (0, 0)),
            pl.BlockSpec((2 * MOBA_WIDTH, d), lambda b, t: (0, 0)),
            pl.BlockSpec((tm, k_width), lambda b, t: (t % n_var, 0)),
        ],
        out_specs=[
            pl.BlockSpec((1, tm, 3 * RET_WIDTH), lambda b, t: (b, t, 0)),
            pl.BlockSpec((1, tm, RET_WIDTH), lambda b, t: (b, t, 0)),
            pl.BlockSpec((1, tm, k_width), lambda b, t: (b, t, 0)),
            pl.BlockSpec((1, tm // MOBA_BLOCK, MOBA_WIDTH, MOBA_BLOCK), lambda b, t: (b, t, 0, 0)),
            pl.BlockSpec((1, tm // MOBA_BLOCK, MOBA_WIDTH, MOBA_BLOCK), lambda b, t: (b, t, 0, 0)),
        ],
        out_shape=[
            jax.ShapeDtypeStruct((bsz, s_len, 3 * RET_WIDTH), BF16),
            jax.ShapeDtypeStruct((bsz, s_len, RET_WIDTH), F32),
            jax.ShapeDtypeStruct((bsz, s_len, k_width), BF16),
            jax.ShapeDtypeStruct((bsz, nb, MOBA_WIDTH, MOBA_BLOCK), BF16),
            jax.ShapeDtypeStruct((bsz, nb, MOBA_WIDTH, MOBA_BLOCK), BF16),
        ],
        compiler_params=pltpu.CompilerParams(
            dimension_semantics=("parallel", "parallel"),
            vmem_limit_bytes=_vmem_limit(est)),
        name="inproj",
    )(x, w_nat, w_tr, kpat)


def _retention_tables():
    h = np.arange(RET_HEADS, dtype=np.float64)
    log_g = np.log1p(-np.exp2(-5.0 - h))
    pos = np.arange(RET_CHUNK, dtype=np.float64)
    diff = pos[:, None] - pos[None, :]
    scale = RET_HEAD_DIM ** -0.5
    intra = np.where(diff >= 0.0, np.exp(log_g[:, None, None] * np.maximum(diff, 0.0)), 0.0) * scale
    k_to_end = np.exp(log_g[:, None] * (RET_CHUNK - 1.0 - pos)[None, :]) * scale
    q_from_start = np.exp(log_g[:, None] * (pos + 1.0)[None, :])
    chunk_decay = np.exp(log_g * RET_CHUNK)
    bcast = lambda t: np.broadcast_to(t[:, :, None], (RET_HEADS, RET_CHUNK, RET_HEAD_DIM))
    return (jnp.asarray(intra, F32), jnp.asarray(bcast(k_to_end), F32),
            jnp.asarray(bcast(q_from_start), F32), jnp.asarray(chunk_decay, F32))


def _retention_kernel(cd_ref, qkv_ref, rg_ref, gain_ref, dec_ref, kte_ref, qfs_ref, o_ref, state_ref):
    e, c = RET_HEAD_DIM, RET_CHUNK

    @pl.when(pl.program_id(1) == 0)
    def _():
        state_ref[...] = jnp.zeros_like(state_ref)

    heads = range(RET_HEADS)
    cols = lambda part, h: slice((part * RET_HEADS + h) * e, (part * RET_HEADS + h + 1) * e)
    state = [state_ref[h] for h in heads]
    for j in range(RET_CHUNKS_PER_STEP):
        rows = slice(j * c, (j + 1) * c)
        q = [qkv_ref[0, rows, cols(0, h)] for h in heads]
        k = [qkv_ref[0, rows, cols(1, h)] for h in heads]
        v = [qkv_ref[0, rows, cols(2, h)] for h in heads]
        raw = [_nt_dot(q[h], k[h]) for h in heads]
        cross = [jnp.dot(q[h], state[h].astype(BF16), preferred_element_type=F32) for h in heads]
        scores = [(raw[h] * dec_ref[h]).astype(BF16) for h in heads]
        k_dec = [(k[h].astype(F32) * kte_ref[h]).T.astype(BF16) for h in heads]
        intra = [jnp.dot(scores[h], v[h], preferred_element_type=F32) for h in heads]
        kv = [jnp.dot(k_dec[h], v[h], preferred_element_type=F32) for h in heads]
        state = [state[h] * cd_ref[h] + kv[h] for h in heads]
        for h in heads:
            y = intra[h] + cross[h] * qfs_ref[h]
            mu = jnp.mean(y, axis=-1, keepdims=True)
            d = y - mu
            var = jnp.mean(d * d, axis=-1, keepdims=True)
            yn = d * lax.rsqrt(var + NORM_EPS) * gain_ref[:, h * e:(h + 1) * e]
            g = rg_ref[0, rows, h * e:(h + 1) * e]
            o_ref[0, rows, h * e:(h + 1) * e] = (g * jax.nn.sigmoid(g) * yn).astype(o_ref.dtype)
    for h in heads:
        state_ref[h] = state[h]


def _retention(rqkv, rg, gain, tables):
    bsz, s_len, _ = rqkv.shape
    dec, kte, qfs, cd = tables
    c, e = RET_CHUNK * RET_CHUNKS_PER_STEP, RET_HEAD_DIM
    assert s_len % c == 0
    full = lambda a: pl.BlockSpec(a.shape, lambda b, n: (0,) * a.ndim)
    return pl.pallas_call(
        _retention_kernel,
        grid=(bsz, s_len // c),
        in_specs=[
            pl.BlockSpec(memory_space=pltpu.SMEM),
            pl.BlockSpec((1, c, 3 * RET_WIDTH), lambda b, n: (b, n, 0)),
            pl.BlockSpec((1, c, RET_WIDTH), lambda b, n: (b, n, 0)),
            full(gain), full(dec), full(kte), full(qfs),
        ],
        out_specs=pl.BlockSpec((1, c, RET_WIDTH), lambda b, n: (b, n, 0)),
        out_shape=jax.ShapeDtypeStruct((bsz, s_len, RET_WIDTH), BF16),
        scratch_shapes=[pltpu.VMEM((RET_HEADS, e, e), F32)],
        compiler_params=pltpu.CompilerParams(
            dimension_semantics=("parallel", "arbitrary")),
        name="retention",
    )(cd, rqkv, rg, gain, dec, kte, qfs)


def _alibi_slopes():
    slopes = np.exp2(-8.0 * (np.arange(MOBA_HEADS, dtype=np.float64) + 1.0) / MOBA_HEADS)
    return jnp.asarray(slopes, F32)


def _moba_past_tiles(n_b):
    ceil_div = lambda a: -(-a // MOBA_GROUP)
    totals = {ceil_div(s) + ceil_div(n_b - 1 - s) for s in range(n_b // 2)}
    assert len(totals) == 1 and n_b % 2 == 0
    return totals.pop()


def _moba_kernel(slope_ref, qa_ref, qb_ref, k_ref, vt_ref, oa_ref, ob_ref,
                 kmean_ref, qtop_ref, aux_ref, m_ref, acc_ref):
    blk, hd, grp = MOBA_BLOCK, MOBA_HEAD_DIM, MOBA_GROUP
    kl, aux_rows, win, sp = MOBA_KEY_LANES, MOBA_AUX_ROWS, MOBA_WINDOW, MOBA_SPLIT
    tile = grp * blk
    n_b = vt_ref.shape[1]
    n_past = _moba_past_tiles(n_b)
    pair = pl.program_id(1)
    step = pl.program_id(2)
    own = (step, n_b - 1 - step)
    n_past_first = (step + grp - 1) // grp
    q_refs, o_refs = (qa_ref, qb_ref), (oa_ref, ob_ref)

    @pl.when(step == 0)
    def _():
        for b in range(n_b):
            kb = k_ref[0, b * blk:(b + 1) * blk, :].astype(F32)
            kmean_ref[b:b + 1, :] = jnp.sum(kb, axis=0, keepdims=True) * (1.0 / blk)

    jidx = lax.broadcasted_iota(jnp.int32, (n_b, blk), 0)
    causal = (lax.broadcasted_iota(jnp.int32, (blk, blk), 0)
              <= lax.broadcasted_iota(jnp.int32, (blk, blk), 1))
    row8 = lax.broadcasted_iota(jnp.int32, (8, blk), 0)
    r_q8 = lax.broadcasted_iota(jnp.int32, (8, blk), 1).astype(F32)
    zero_rows = lambda n: jnp.zeros((n, blk), BF16)
    ones_rows = lambda n: (lax.broadcasted_iota(jnp.int32, (MOBA_SUM_ROWS, n), 0) == 0).astype(BF16)

    def bf16_pieces(v):
        pieces = []
        for _ in range(sp):
            piece = v.astype(BF16).astype(F32)
            pieces.append(piece)
            v = v - piece
        return pieces

    def col_max(u):
        return jnp.max(jnp.max(u.reshape(-1, blk, blk), axis=0), axis=0, keepdims=True)

    alibi_rows = []
    for hl in range(2):
        c = slope_ref[2 * pair + hl] * LOG2E
        pieces = bf16_pieces(jnp.where(row8 < sp, c, -c * r_q8))
        rows = jnp.zeros((8, blk), F32)
        for s in range(sp):
            rows = jnp.where((row8 == s) | (row8 == sp + s), pieces[s], rows)
        alibi_rows.append(rows)

    chains = [(o, hl) for o in range(2) for hl in range(2)]
    own_scores, gates = {}, {}
    for o, hl in chains:
        q_top = q_refs[o][0, 0, hl * hd:(hl + 1) * hd, :]
        qtop_ref[o, hl] = q_top
        aux_own = jnp.concatenate([alibi_rows[hl], jnp.zeros((aux_rows - 8, blk), F32)], axis=0)
        q_own = jnp.concatenate([q_top, aux_own.astype(BF16), zero_rows(kl - hd - aux_rows)], axis=0)
        kd = k_ref[0, pl.ds(pl.multiple_of(own[o] * blk, blk), blk), hl * kl:(hl + 1) * kl]
        own_scores[o, hl] = jnp.dot(kd, q_own, preferred_element_type=F32)
        kmean = kmean_ref[:, hl * kl:(hl + 1) * kl].astype(BF16)
        q_gate = jnp.concatenate([q_top, zero_rows(kl - hd)], axis=0)
        gates[o, hl] = jnp.dot(kmean, q_gate, preferred_element_type=F32)

    for o, hl in chains:
        u = jnp.where(causal, own_scores[o, hl], NEG_INF)
        m = col_max(u)
        p = jnp.exp2(u - m)
        m_ref[o, hl] = m
        vt = jnp.concatenate([vt_ref[0, own[o], hl * hd:(hl + 1) * hd, :], ones_rows(blk)], axis=0)
        acc_ref[o, hl] = jnp.dot(vt, p.astype(BF16), preferred_element_type=F32)

        past = jidx < own[o]
        gate = gates[o, hl]
        taken = jnp.zeros((n_b, blk), jnp.bool_)
        for _ in range(MOBA_TOP_K):
            cand = past & jnp.logical_not(taken)
            best = jnp.max(jnp.where(cand, gate, -jnp.inf), axis=0, keepdims=True)
            pick = cand & (gate == best)
            first = jnp.min(jnp.where(pick, jidx, n_b), axis=0, keepdims=True)
            taken = taken | (jidx == first)
        c = slope_ref[2 * pair + hl] * LOG2E
        block_off = (own[o] - jidx).astype(F32) * (-float(blk) * c)
        bias_pieces = bf16_pieces(jnp.where(taken, block_off, NEG_INF))
        for w in range(n_b // win):
            rows = [alibi_rows[hl]] + [piece[w * win:(w + 1) * win, :] for piece in bias_pieces]
            aux_ref[o, hl, w] = jnp.concatenate(rows, axis=0).astype(BF16)

    tiles = []
    for k in range(n_past):
        which = jnp.where(k >= n_past_first, 1, 0)
        tiles.append((which, k - which * n_past_first))

    def unit_scores(k, hl):
        which, t = tiles[k]
        q_op = jnp.concatenate([qtop_ref[which, hl], aux_ref[which, hl, (t * grp) // win],
                                zero_rows(kl - hd - aux_rows)], axis=0)
        k_rows = k_ref[0, pl.ds(pl.multiple_of(t * tile, tile), tile), hl * kl:(hl + 1) * kl]
        s = jnp.dot(k_rows, q_op, preferred_element_type=F32)
        return s, col_max(s)

    def unit_update(k, hl, s, cmax):
        which, t = tiles[k]
        m = m_ref[which, hl]
        m_new = jnp.maximum(m, cmax)
        alpha = jnp.exp2(m - m_new)
        p = jnp.exp2(s - m_new)
        m_ref[which, hl] = m_new
        vt = jnp.concatenate(
            [vt_ref[0, t * grp + c, hl * hd:(hl + 1) * hd, :] for c in range(grp)], axis=1)
        vt = jnp.concatenate([vt, ones_rows(tile)], axis=0)
        acc_ref[which, hl] = alpha * acc_ref[which, hl] + jnp.dot(
            vt, p.astype(BF16), preferred_element_type=F32)

    units = [(k, hl) for k in range(n_past) for hl in range(2)]
    pending = []
    for idx in range(len(units) + MOBA_LOOKAHEAD):
        if idx < len(units):
            pending.append(unit_scores(*units[idx]))
        if idx >= MOBA_LOOKAHEAD:
            unit_update(*units[idx - MOBA_LOOKAHEAD], *pending.pop(0))

    for o in range(2):
        outs = [acc_ref[o, hl, :hd, :] * (1.0 / acc_ref[o, hl, hd:hd + 1, :]) for hl in range(2)]
        o_refs[o][0] = jnp.concatenate(outs, axis=0).T.astype(oa_ref.dtype)


def _moba(mqt, mk, mvt, slopes):
    bsz, n_b, width, blk = mqt.shape
    s_len = mk.shape[1]
    pw = 2 * MOBA_HEAD_DIM
    kw = 2 * MOBA_KEY_LANES
    tile = MOBA_GROUP * blk
    half = n_b // 2
    est = 2 * s_len * (kw + pw) * 2 + 4 * 2 * 4 * tile * blk * 4
    out_half = jax.ShapeDtypeStruct((bsz, half * blk, width), BF16)
    return pl.pallas_call(
        _moba_kernel,
        grid=(bsz, width // pw, half),
        in_specs=[
            pl.BlockSpec(memory_space=pltpu.SMEM),
            pl.BlockSpec((1, 1, pw, blk), lambda b, p, s: (b, s, p, 0)),
            pl.BlockSpec((1, 1, pw, blk), lambda b, p, s: (b, n_b - 1 - s, p, 0)),
            pl.BlockSpec((1, s_len, kw), lambda b, p, s: (b, 0, p)),
            pl.BlockSpec((1, n_b, pw, blk), lambda b, p, s: (b, 0, p, 0)),
        ],
        out_specs=[pl.BlockSpec((1, blk, pw), lambda b, p, s: (b, s, p)),
                   pl.BlockSpec((1, blk, pw), lambda b, p, s: (b, half - 1 - s, p))],
        out_shape=[out_half, out_half],
        scratch_shapes=[pltpu.VMEM((n_b, kw), F32),
                        pltpu.VMEM((2, 2, MOBA_HEAD_DIM, blk), BF16),
                        pltpu.VMEM((2, 2, n_b // MOBA_WINDOW, MOBA_AUX_ROWS, blk), BF16),
                        pltpu.VMEM((2, 2, 1, blk), F32),
                        pltpu.VMEM((2, 2, MOBA_HEAD_DIM + MOBA_SUM_ROWS, blk), F32)],
        compiler_params=pltpu.CompilerParams(
            dimension_semantics=("parallel", "parallel", "arbitrary"),
            vmem_limit_bytes=_vmem_limit(est)),
        name="moba",
    )(slopes, mqt, mqt, mk, mvt)


def _mix_ffn_kernel(alpha, lo_tiles, x_ref, ret_ref, mlo_ref, mhi_ref, wr_ref, wm_ref, g1_ref, b1_ref,
                    wg_ref, wu_ref, wd_ref, g2_ref, b2_ref, o_ref):
    moba = jnp.where(pl.program_id(1) < lo_tiles, mlo_ref[0], mhi_ref[0])
    dot = functools.partial(jnp.dot, preferred_element_type=F32)
    rows = x_ref.shape[1] // FFN_SUBTILES
    part = lambda a, i: a[i * rows:(i + 1) * rows]
    mixed = [dot(part(ret_ref[0], i), wr_ref[...]) + dot(part(moba, i), wm_ref[...])
             for i in range(FFN_SUBTILES)]
    h, act, ffn = {}, {}, {}
    for i in range(FFN_SUBTILES + 2):
        if i < FFN_SUBTILES:
            h[i] = _layer_norm(alpha * part(x_ref[0], i) + mixed[i], g1_ref[...], b1_ref[...])
            hb = h[i].astype(BF16)
            gate = dot(hb, wg_ref[...])
            up = dot(hb, wu_ref[...])
            act[i] = (gate * jax.nn.sigmoid(gate) * up).astype(BF16)
        if 1 <= i <= FFN_SUBTILES:
            ffn[i - 1] = dot(act[i - 1], wd_ref[...])
        if i >= 2:
            j = i - 2
            o_ref[0, j * rows:(j + 1) * rows, :] = _layer_norm(alpha * h[j] + ffn[j], g2_ref[...], b2_ref[...])


def _mix_ffn(x, ret, moba_lo, moba_hi, w_ret, w_moba, g1, b1, w_gate, w_up, w_down, g2, b2, alpha):
    bsz, s_len, d = x.shape
    d_ff = w_gate.shape[1]
    tm = TOKEN_TILE
    lo_tiles = moba_lo.shape[1] // tm
    assert moba_lo.shape[1] % tm == 0 and moba_hi.shape[1] == s_len - moba_lo.shape[1]
    row = lambda w: pl.BlockSpec((1, tm, w), lambda b_, t: (b_, t, 0))
    lo_row = pl.BlockSpec((1, tm, MOBA_WIDTH), lambda b_, t: (b_, jnp.minimum(t, lo_tiles - 1), 0))
    hi_row = pl.BlockSpec((1, tm, MOBA_WIDTH), lambda b_, t: (b_, jnp.maximum(t - lo_tiles, 0), 0))
    full = lambda a: pl.BlockSpec(a.shape, lambda b_, t: (0, 0), pipeline_mode=pl.Buffered(1))
    est = ((3 * d * d_ff + d * d) * 2 + 2 * 2 * tm * d * 4 + 2 * 3 * tm * RET_WIDTH * 2
           + tm * d_ff * (4 + 4 + 2) + 3 * tm * d * 4)
    return pl.pallas_call(
        functools.partial(_mix_ffn_kernel, alpha, lo_tiles),
        grid=(bsz, s_len // tm),
        in_specs=[row(d), row(RET_WIDTH), lo_row, hi_row, full(w_ret), full(w_moba), full(g1), full(b1),
                  full(w_gate), full(w_up), full(w_down), full(g2), full(b2)],
        out_specs=row(d),
        out_shape=jax.ShapeDtypeStruct((bsz, s_len, d), F32),
        compiler_params=pltpu.CompilerParams(
            dimension_semantics=("parallel", "parallel"),
            vmem_limit_bytes=_vmem_limit(est)),
        name="mix_ffn",
    )(x, ret, moba_lo, moba_hi, w_ret, w_moba, g1, b1, w_gate, w_up, w_down, g2, b2)


def kernel(x, w_in, ret_gn_gain, w_out, ln1_g, ln1_b, w_gate, w_up, w_down, ln2_g, ln2_b):
    bsz, s_len, d = x.shape
    depth = w_in.shape[0]
    assert s_len % math.lcm(RET_CHUNK, MOBA_BLOCK, TOKEN_TILE) == 0
    assert w_in.shape[2] == 4 * RET_WIDTH + 3 * MOBA_WIDTH
    alpha = (2 * depth) ** 0.25
    ret_tables = _retention_tables()
    slopes = _alibi_slopes()
    r4 = 4 * RET_WIDTH
    h = x
    for layer in range(depth):
        w = w_in[layer]
        w_mk = w[:, r4 + MOBA_WIDTH:r4 + 2 * MOBA_WIDTH].reshape(d, MOBA_HEADS, MOBA_HEAD_DIM)
        w_mk = jnp.pad(w_mk, ((0, 0), (0, 0), (0, MOBA_KEY_LANES - MOBA_HEAD_DIM)))
        w_nat = jnp.concatenate([w[:, :r4], w_mk.reshape(d, MOBA_HEADS * MOBA_KEY_LANES)], axis=1).astype(BF16)
        w_tr = jnp.concatenate([w[:, r4:r4 + MOBA_WIDTH], w[:, r4 + 2 * MOBA_WIDTH:]], axis=1).T.astype(BF16)
        rqkv, rg, mk, mqt, mvt = _inproj(h, w_nat, w_tr)
        ret = _retention(rqkv, rg, ret_gn_gain[layer][None, :], ret_tables)
        moba_lo, moba_hi = _moba(mqt, mk, mvt, slopes)
        wo = w_out[layer].astype(BF16)
        h = _mix_ffn(h, ret, moba_lo, moba_hi, wo[:RET_WIDTH], wo[RET_WIDTH:],
                     ln1_g[layer][None, :], ln1_b[layer][None, :],
                     w_gate[layer].astype(BF16), w_up[layer].astype(BF16), w_down[layer].astype(BF16),
                     ln2_g[layer][None, :], ln2_b[layer][None, :], alpha)
    return h
```

```python
import functools
import math

import numpy as np
import jax
import jax.numpy as jnp
from jax import lax
from jax.experimental import pallas as pl
from jax.experimental.pallas import tpu as pltpu

F32 = jnp.float32
BF16 = jnp.bfloat16

RET_HEADS = 4
RET_HEAD_DIM = 128
RET_WIDTH = RET_HEADS * RET_HEAD_DIM
RET_CHUNK = 256
RET_CHUNKS_PER_STEP = 4
MOBA_HEADS = 8
MOBA_HEAD_DIM = 64
MOBA_WIDTH = MOBA_HEADS * MOBA_HEAD_DIM
MOBA_BLOCK = 256
MOBA_TOP_K = 3
MOBA_GROUP = 2
MOBA_LOOKAHEAD = 4
MOBA_SUM_ROWS = 16
MOBA_KEY_LANES = 128
MOBA_WINDOW = 8
MOBA_SPLIT = 3
MOBA_AUX_ROWS = 8 + MOBA_SPLIT * MOBA_WINDOW
LOG2E = 1.4426950408889634
NORM_EPS = 1e-5
NEG_INF = -1e30

V7X_VMEM_BYTES = 64 * 1024 * 1024
LANES = 128
TOKEN_TILE = 512
FFN_TOKEN_TILE = 1024
FFN_SUBTILES = 4


def _vmem_limit(estimate_bytes):
    return int(min(V7X_VMEM_BYTES - 4 * 1024 * 1024, max(estimate_bytes, 16 * 1024 * 1024)))


def _nt_dot(a, b):
    return lax.dot_general(a, b, (((1,), (1,)), ((), ())), preferred_element_type=F32)


def _layer_norm(v, g, b):
    mu = jnp.mean(v, axis=-1, keepdims=True)
    d = v - mu
    var = jnp.mean(d * d, axis=-1, keepdims=True)
    return d * lax.rsqrt(var + NORM_EPS) * g + b


def _key_side_pattern(tm):
    hd, sp, win = MOBA_HEAD_DIM, MOBA_SPLIT, MOBA_WINDOW
    assert win % (tm // MOBA_BLOCK) == 0 and 2 * sp <= 8 and hd + MOBA_AUX_ROWS <= MOBA_KEY_LANES
    pos = np.arange(win * MOBA_BLOCK)
    pat = np.zeros((pos.size, MOBA_HEADS, MOBA_KEY_LANES), np.float32)
    pat[:, :, hd:hd + sp] = (pos % MOBA_BLOCK)[:, None, None]
    pat[:, :, hd + sp:hd + 2 * sp] = 1.0
    for s in range(sp):
        for c in range(win):
            pat[:, :, hd + 8 + s * win + c] = (pos // MOBA_BLOCK == c)[:, None]
    return jnp.asarray(pat.reshape(pos.size, MOBA_HEADS * MOBA_KEY_LANES), BF16)


def _inproj_kernel(x_ref, wn_ref, wt_ref, kpat_ref, rqkv_ref, rg_ref, mk_ref, mqt_ref, mvt_ref):
    x = x_ref[0].astype(BF16)
    nat = jnp.dot(x, wn_ref[...], preferred_element_type=F32)
    rqkv_ref[0] = nat[:, :3 * RET_WIDTH].astype(BF16)
    rg_ref[0] = nat[:, 3 * RET_WIDTH:4 * RET_WIDTH]
    mk_ref[0] = nat[:, 4 * RET_WIDTH:].astype(BF16) + kpat_ref[...]
    tr = _nt_dot(wt_ref[...], x)
    for c in range(x.shape[0] // MOBA_BLOCK):
        cols = slice(c * MOBA_BLOCK, (c + 1) * MOBA_BLOCK)
        mqt_ref[0, c] = (tr[:MOBA_WIDTH, cols] * (MOBA_HEAD_DIM ** -0.5 * LOG2E)).astype(BF16)
        mvt_ref[0, c] = tr[MOBA_WIDTH:, cols].astype(BF16)


def _inproj(x, w_nat, w_tr):
    bsz, s_len, d = x.shape
    tm = TOKEN_TILE
    nb = s_len // MOBA_BLOCK
    n_nat = w_nat.shape[1]
    k_width = MOBA_HEADS * MOBA_KEY_LANES
    kpat = _key_side_pattern(tm)
    n_var = kpat.shape[0] // tm
    est = (2 * tm * d * 4 + 2 * (w_nat.size + w_tr.size) * 2 + 2 * tm * k_width * 2
           + 2 * tm * (3 * RET_WIDTH * 2 + RET_WIDTH * 4 + (k_width + 2 * MOBA_WIDTH) * 2)
           + tm * (n_nat + 2 * MOBA_WIDTH) * 4 * 2)
    return pl.pallas_call(
        _inproj_kernel,
        grid=(bsz, s_len // tm),
        in_specs=[
            pl.BlockSpec((1, tm, d), lambda b, t: (b, t, 0)),
            pl.BlockSpec((d, n_nat), lambda b, t: (0, 0)),
            pl.BlockSpec((2 * MOBA_WIDTH, d), lambda b, t: (0, 0)),
            pl.BlockSpec((tm, k_width), lambda b, t: (t % n_var, 0)),
        ],
        out_specs=[
            pl.BlockSpec((1, tm, 3 * RET_WIDTH), lambda b, t: (b, t, 0)),
            pl.BlockSpec((1, tm, RET_WIDTH), lambda b, t: (b, t, 0)),
            pl.BlockSpec((1, tm, k_width), lambda b, t: (b, t, 0)),
            pl.BlockSpec((1, tm // MOBA_BLOCK, MOBA_WIDTH, MOBA_BLOCK), lambda b, t: (b, t, 0, 0)),
            pl.BlockSpec((1, tm // MOBA_BLOCK, MOBA_WIDTH, MOBA_BLOCK), lambda b, t: (b, t, 0, 0)),
        ],
        out_shape=[
            jax.ShapeDtypeStruct((bsz, s_len, 3 * RET_WIDTH), BF16),
            jax.ShapeDtypeStruct((bsz, s_len, RET_WIDTH), F32),
            jax.ShapeDtypeStruct((bsz, s_len, k_width), BF16),
            jax.ShapeDtypeStruct((bsz, nb, MOBA_WIDTH, MOBA_BLOCK), BF16),
            jax.ShapeDtypeStruct((bsz, nb, MOBA_WIDTH, MOBA_BLOCK), BF16),
        ],
        compiler_params=pltpu.CompilerParams(
            dimension_semantics=("parallel", "parallel"),
            vmem_limit_bytes=_vmem_limit(est)),
        name="inproj",
    )(x, w_nat, w_tr, kpat)


def _retention_tables():
    h = np.arange(RET_HEADS, dtype=np.float64)
    log_g = np.log1p(-np.exp2(-5.0 - h))
    pos = np.arange(RET_CHUNK, dtype=np.float64)
    diff = pos[:, None] - pos[None, :]
    scale = RET_HEAD_DIM ** -0.5
    intra = np.where(diff >= 0.0, np.exp(log_g[:, None, None] * np.maximum(diff, 0.0)), 0.0) * scale
    k_to_end = np.exp(log_g[:, None] * (RET_CHUNK - 1.0 - pos)[None, :]) * scale
    q_from_start = np.exp(log_g[:, None] * (pos + 1.0)[None, :])
    chunk_decay = np.exp(log_g * RET_CHUNK)
    bcast = lambda t: np.broadcast_to(t[:, :, None], (RET_HEADS, RET_CHUNK, RET_HEAD_DIM))
    return (jnp.asarray(intra, F32), jnp.asarray(bcast(k_to_end), F32),
            jnp.asarray(bcast(q_from_start), F32), jnp.asarray(chunk_decay, F32))


def _retention_kernel(cd_ref, qkv_ref, rg_ref, gain_ref, dec_ref, kte_ref, qfs_ref, o_ref, state_ref):
    e, c = RET_HEAD_DIM, RET_CHUNK

    @pl.when(pl.program_id(1) == 0)
    def _():
        state_ref[...] = jnp.zeros_like(state_ref)

    heads = range(RET_HEADS)
    cols = lambda part, h: slice((part * RET_HEADS + h) * e, (part * RET_HEADS + h + 1) * e)
    state = [state_ref[h] for h in heads]
    for j in range(RET_CHUNKS_PER_STEP):
        rows = slice(j * c, (j + 1) * c)
        q = [qkv_ref[0, rows, cols(0, h)] for h in heads]
        k = [qkv_ref[0, rows, cols(1, h)] for h in heads]
        v = [qkv_ref[0, rows, cols(2, h)] for h in heads]
        raw = [_nt_dot(q[h], k[h]) for h in heads]
        cross = [jnp.dot(q[h], state[h].astype(BF16), preferred_element_type=F32) for h in heads]
        scores = [(raw[h] * dec_ref[h]).astype(BF16) for h in heads]
        k_dec = [(k[h].astype(F32) * kte_ref[h]).T.astype(BF16) for h in heads]
        intra = [jnp.dot(scores[h], v[h], preferred_element_type=F32) for h in heads]
        kv = [jnp.dot(k_dec[h], v[h], preferred_element_type=F32) for h in heads]
        state = [state[h] * cd_ref[h] + kv[h] for h in heads]
        for h in heads:
            y = intra[h] + cross[h] * qfs_ref[h]
            mu = jnp.mean(y, axis=-1, keepdims=True)
            d = y - mu
            var = jnp.mean(d * d, axis=-1, keepdims=True)
            yn = d * lax.rsqrt(var + NORM_EPS) * gain_ref[:, h * e:(h + 1) * e]
            g = rg_ref[0, rows, h * e:(h + 1) * e]
            o_ref[0, rows, h * e:(h + 1) * e] = (g * jax.nn.sigmoid(g) * yn).astype(o_ref.dtype)
    for h in heads:
        state_ref[h] = state[h]


def _retention(rqkv, rg, gain, tables):
    bsz, s_len, _ = rqkv.shape
    dec, kte, qfs, cd = tables
    c, e = RET_CHUNK * RET_CHUNKS_PER_STEP, RET_HEAD_DIM
    assert s_len % c == 0
    full = lambda a: pl.BlockSpec(a.shape, lambda b, n: (0,) * a.ndim)
    return pl.pallas_call(
        _retention_kernel,
        grid=(bsz, s_len // c),
        in_specs=[
            pl.BlockSpec(memory_space=pltpu.SMEM),
            pl.BlockSpec((1, c, 3 * RET_WIDTH), lambda b, n: (b, n, 0)),
            pl.BlockSpec((1, c, RET_WIDTH), lambda b, n: (b, n, 0)),
            full(gain), full(dec), full(kte), full(qfs),
        ],
        out_specs=pl.BlockSpec((1, c, RET_WIDTH), lambda b, n: (b, n, 0)),
        out_shape=jax.ShapeDtypeStruct((bsz, s_len, RET_WIDTH), BF16),
        scratch_shapes=[pltpu.VMEM((RET_HEADS, e, e), F32)],
        compiler_params=pltpu.CompilerParams(
            dimension_semantics=("parallel", "arbitrary")),
        name="retention",
    )(cd, rqkv, rg, gain, dec, kte, qfs)


def _alibi_slopes():
    slopes = np.exp2(-8.0 * (np.arange(MOBA_HEADS, dtype=np.float64) + 1.0) / MOBA_HEADS)
    return jnp.asarray(slopes, F32)


def _moba_past_tiles(n_b):
    ceil_div = lambda a: -(-a // MOBA_GROUP)
    totals = {ceil_div(s) + ceil_div(n_b - 1 - s) for s in range(n_b // 2)}
    assert len(totals) == 1 and n_b % 2 == 0
    return totals.pop()


def _moba_kernel(slope_ref, qa_ref, qb_ref, k_ref, vt_ref, oa_ref, ob_ref,
                 kmean_ref, qtop_ref, aux_ref, m_ref, acc_ref):
    blk, hd, grp = MOBA_BLOCK, MOBA_HEAD_DIM, MOBA_GROUP
    kl, aux_rows, win, sp = MOBA_KEY_LANES, MOBA_AUX_ROWS, MOBA_WINDOW, MOBA_SPLIT
    tile = grp * blk
    n_b = vt_ref.shape[1]
    n_past = _moba_past_tiles(n_b)
    pair = pl.program_id(1)
    step = pl.program_id(2)
    own = (step, n_b - 1 - step)
    n_past_first = (step + grp - 1) // grp
    q_refs, o_refs = (qa_ref, qb_ref), (oa_ref, ob_ref)

    @pl.when(step == 0)
    def _():
        for b in range(n_b):
            kb = k_ref[0, b * blk:(b + 1) * blk, :].astype(F32)
            kmean_ref[b:b + 1, :] = jnp.sum(kb, axis=0, keepdims=True) * (1.0 / blk)

    jidx = lax.broadcasted_iota(jnp.int32, (n_b, blk), 0)
    causal = (lax.broadcasted_iota(jnp.int32, (blk, blk), 0)
              <= lax.broadcasted_iota(jnp.int32, (blk, blk), 1))
    row8 = lax.broadcasted_iota(jnp.int32, (8, blk), 0)
    r_q8 = lax.broadcasted_iota(jnp.int32, (8, blk), 1).astype(F32)
    zero_rows = lambda n: jnp.zeros((n, blk), BF16)
    ones_rows = lambda n: (lax.broadcasted_iota(jnp.int32, (MOBA_SUM_ROWS, n), 0) == 0).astype(BF16)

    def bf16_pieces(v):
        pieces = []
        for _ in range(sp):
            piece = v.astype(BF16).astype(F32)
            pieces.append(piece)
            v = v - piece
        return pieces

    def col_max(u):
        return jnp.max(jnp.max(u.reshape(-1, blk, blk), axis=0), axis=0, keepdims=True)

    alibi_rows = []
    for hl in range(2):
        c = slope_ref[2 * pair + hl] * LOG2E
        pieces = bf16_pieces(jnp.where(row8 < sp, c, -c * r_q8))
        rows = jnp.zeros((8, blk), F32)
        for s in range(sp):
            rows = jnp.where((row8 == s) | (row8 == sp + s), pieces[s], rows)
        alibi_rows.append(rows)

    chains = [(o, hl) for o in range(2) for hl in range(2)]
    own_scores, gates = {}, {}
    for o, hl in chains:
        q_top = q_refs[o][0, 0, hl * hd:(hl + 1) * hd, :]
        qtop_ref[o, hl] = q_top
        aux_own = jnp.concatenate([alibi_rows[hl], jnp.zeros((aux_rows - 8, blk), F32)], axis=0)
        q_own = jnp.concatenate([q_top, aux_own.astype(BF16), zero_rows(kl - hd - aux_rows)], axis=0)
        kd = k_ref[0, pl.ds(pl.multiple_of(own[o] * blk, blk), blk), hl * kl:(hl + 1) * kl]
        own_scores[o, hl] = jnp.dot(kd, q_own, preferred_element_type=F32)
        kmean = kmean_ref[:, hl * kl:(hl + 1) * kl].astype(BF16)
        q_gate = jnp.concatenate([q_top, zero_rows(kl - hd)], axis=0)
        gates[o, hl] = jnp.dot(kmean, q_gate, preferred_element_type=F32)

    for o, hl in chains:
        u = jnp.where(causal, own_scores[o, hl], NEG_INF)
        m = col_max(u)
        p = jnp.exp2(u - m)
        m_ref[o, hl] = m
        vt = jnp.concatenate([vt_ref[0, own[o], hl * hd:(hl + 1) * hd, :], ones_rows(blk)], axis=0)
        acc_ref[o, hl] = jnp.dot(vt, p.astype(BF16), preferred_element_type=F32)

        past = jidx < own[o]
        gate = gates[o, hl]
        taken = jnp.zeros((n_b, blk), jnp.bool_)
        for _ in range(MOBA_TOP_K):
            cand = past & jnp.logical_not(taken)
            best = jnp.max(jnp.where(cand, gate, -jnp.inf), axis=0, keepdims=True)
            pick = cand & (gate == best)
            first = jnp.min(jnp.where(pick, jidx, n_b), axis=0, keepdims=True)
            taken = taken | (jidx == first)
        c = slope_ref[2 * pair + hl] * LOG2E
        block_off = (own[o] - jidx).astype(F32) * (-float(blk) * c)
        bias_pieces = bf16_pieces(jnp.where(taken, block_off, NEG_INF))
        for w in range(n_b // win):
            rows = [alibi_rows[hl]] + [piece[w * win:(w + 1) * win, :] for piece in bias_pieces]
            aux_ref[o, hl, w] = jnp.concatenate(rows, axis=0).astype(BF16)

    tiles = []
    for k in range(n_past):
        which = jnp.where(k >= n_past_first, 1, 0)
        tiles.append((which, k - which * n_past_first))

    def unit_scores(k, hl):
        which, t = tiles[k]
        q_op = jnp.concatenate([qtop_ref[which, hl], aux_ref[which, hl, (t * grp) // win],
                                zero_rows(kl - hd - aux_rows)], axis=0)
        k_rows = k_ref[0, pl.ds(pl.multiple_of(t * tile, tile), tile), hl * kl:(hl + 1) * kl]
        s = jnp.dot(k_rows, q_op, preferred_element_type=F32)
        return s, col_max(s)

    def unit_update(k, hl, s, cmax):
        which, t = tiles[k]
        m = m_ref[which, hl]
        m_new = jnp.maximum(m, cmax)
        alpha = jnp.exp2(m - m_new)
        p = jnp.exp2(s - m_new)
        m_ref[which, hl] = m_new
        vt = jnp.concatenate(
            [vt_ref[0, t * grp + c, hl * hd:(hl + 1) * hd, :] for c in range(grp)], axis=1)
        vt = jnp.concatenate([vt, ones_rows(tile)], axis=0)
        acc_ref[which, hl] = alpha * acc_ref[which, hl] + jnp.dot(
            vt, p.astype(BF16), preferred_element_type=F32)

    units = [(k, hl) for k in range(n_past) for hl in range(2)]
    pending = []
    for idx in range(len(units) + MOBA_LOOKAHEAD):
        if idx < len(units):
            pending.append(unit_scores(*units[idx]))
        if idx >= MOBA_LOOKAHEAD:
            unit_update(*units[idx - MOBA_LOOKAHEAD], *pending.pop(0))

    for o in range(2):
        outs = [acc_ref[o, hl, :hd, :] * (1.0 / acc_ref[o, hl, hd:hd + 1, :]) for hl in range(2)]
        o_refs[o][0] = jnp.concatenate(outs, axis=0).T.astype(oa_ref.dtype)


def _moba(mqt, mk, mvt, slopes):
    bsz, n_b, width, blk = mqt.shape
    s_len = mk.shape[1]
    pw = 2 * MOBA_HEAD_DIM
    kw = 2 * MOBA_KEY_LANES
    tile = MOBA_GROUP * blk
    half = n_b // 2
    est = 2 * s_len * (kw + pw) * 2 + 4 * 2 * 4 * tile * blk * 4
    out_half = jax.ShapeDtypeStruct((bsz, half * blk, width), BF16)
    return pl.pallas_call(
        _moba_kernel,
        grid=(bsz, width // pw, half),
        in_specs=[
            pl.BlockSpec(memory_space=pltpu.SMEM),
            pl.BlockSpec((1, 1, pw, blk), lambda b, p, s: (b, s, p, 0)),
            pl.BlockSpec((1, 1, pw, blk), lambda b, p, s: (b, n_b - 1 - s, p, 0)),
            pl.BlockSpec((1, s_len, kw), lambda b, p, s: (b, 0, p)),
            pl.BlockSpec((1, n_b, pw, blk), lambda b, p, s: (b, 0, p, 0)),
        ],
        out_specs=[pl.BlockSpec((1, blk, pw), lambda b, p, s: (b, s, p)),
                   pl.BlockSpec((1, blk, pw), lambda b, p, s: (b, half - 1 - s, p))],
        out_shape=[out_half, out_half],
        scratch_shapes=[pltpu.VMEM((n_b, kw), F32),
                        pltpu.VMEM((2, 2, MOBA_HEAD_DIM, blk), BF16),
                        pltpu.VMEM((2, 2, n_b // MOBA_WINDOW, MOBA_AUX_ROWS, blk), BF16),
                        pltpu.VMEM((2, 2, 1, blk), F32),
                        pltpu.VMEM((2, 2, MOBA_HEAD_DIM + MOBA_SUM_ROWS, blk), F32)],
        compiler_params=pltpu.CompilerParams(
            dimension_semantics=("parallel", "parallel", "arbitrary"),
            vmem_limit_bytes=_vmem_limit(est)),
        name="moba",
    )(slopes, mqt, mqt, mk, mvt)


def _mix_ffn_kernel(alpha, lo_tiles, x_ref, ret_ref, mlo_ref, mhi_ref, wr_ref, wm_ref, g1_ref, b1_ref,
                    wg_ref, wu_ref, wd_ref, g2_ref, b2_ref, o_ref):
    moba = jnp.where(pl.program_id(1) < lo_tiles, mlo_ref[0], mhi_ref[0])
    dot = functools.partial(jnp.dot, preferred_element_type=F32)
    rows = x_ref.shape[1] // FFN_SUBTILES
    part = lambda a, i: a[i * rows:(i + 1) * rows]
    mixed = [dot(part(ret_ref[0], i), wr_ref[...]) + dot(part(moba, i), wm_ref[...])
             for i in range(FFN_SUBTILES)]
    h, act, ffn = {}, {}, {}
    for i in range(FFN_SUBTILES + 2):
        if i < FFN_SUBTILES:
            h[i] = _layer_norm(alpha * part(x_ref[0], i) + mixed[i], g1_ref[...], b1_ref[...])
            hb = h[i].astype(BF16)
            gate = dot(hb, wg_ref[...])
            up = dot(hb, wu_ref[...])
            act[i] = (gate * jax.nn.sigmoid(gate) * up).astype(BF16)
        if 1 <= i <= FFN_SUBTILES:
            ffn[i - 1] = dot(act[i - 1], wd_ref[...])
        if i >= 2:
            j = i - 2
            o_ref[0, j * rows:(j + 1) * rows, :] = _layer_norm(alpha * h[j] + ffn[j], g2_ref[...], b2_ref[...])


def _mix_ffn(x, ret, moba_lo, moba_hi, w_ret, w_moba, g1, b1, w_gate, w_up, w_down, g2, b2, alpha):
    bsz, s_len, d = x.shape
    d_ff = w_gate.shape[1]
    tm = FFN_TOKEN_TILE
    sub = tm // FFN_SUBTILES
    lo_tiles = moba_lo.shape[1] // tm
    assert moba_lo.shape[1] % tm == 0 and moba_hi.shape[1] == s_len - moba_lo.shape[1]
    row = lambda w: pl.BlockSpec((1, tm, w), lambda b_, t: (b_, t, 0))
    lo_row = pl.BlockSpec((1, tm, MOBA_WIDTH), lambda b_, t: (b_, jnp.minimum(t, lo_tiles - 1), 0))
    hi_row = pl.BlockSpec((1, tm, MOBA_WIDTH), lambda b_, t: (b_, jnp.maximum(t - lo_tiles, 0), 0))
    full = lambda a: pl.BlockSpec(a.shape, lambda b_, t: (0, 0), pipeline_mode=pl.Buffered(1))
    est = ((3 * d * d_ff + d * d) * 2 + 2 * 2 * tm * d * 4 + 2 * 3 * tm * RET_WIDTH * 2
           + 2 * sub * d_ff * (4 + 4 + 2) + FFN_SUBTILES * sub * d * 4 * 2)
    return pl.pallas_call(
        functools.partial(_mix_ffn_kernel, alpha, lo_tiles),
        grid=(bsz, s_len // tm),
        in_specs=[row(d), row(RET_WIDTH), lo_row, hi_row, full(w_ret), full(w_moba), full(g1), full(b1),
                  full(w_gate), full(w_up), full(w_down), full(g2), full(b2)],
        out_specs=row(d),
        out_shape=jax.ShapeDtypeStruct((bsz, s_len, d), F32),
        compiler_params=pltpu.CompilerParams(
            dimension_semantics=("parallel", "parallel"),
            vmem_limit_bytes=_vmem_limit(est)),
        name="mix_ffn",
    )(x, ret, moba_lo, moba_hi, w_ret, w_moba, g1, b1, w_gate, w_up, w_down, g2, b2)


def kernel(x, w_in, ret_gn_gain, w_out, ln1_g, ln1_b, w_gate, w_up, w_down, ln2_g, ln2_b):
    bsz, s_len, d = x.shape
    depth = w_in.shape[0]
    assert s_len % math.lcm(RET_CHUNK, MOBA_BLOCK, TOKEN_TILE, FFN_TOKEN_TILE) == 0
    assert w_in.shape[2] == 4 * RET_WIDTH + 3 * MOBA_WIDTH
    alpha = (2 * depth) ** 0.25
    ret_tables = _retention_tables()
    slopes = _alibi_slopes()
    r4 = 4 * RET_WIDTH
    h = x
    for layer in range(depth):
        w = w_in[layer]
        w_mk = w[:, r4 + MOBA_WIDTH:r4 + 2 * MOBA_WIDTH].reshape(d, MOBA_HEADS, MOBA_HEAD_DIM)
        w_mk = jnp.pad(w_mk, ((0, 0), (0, 0), (0, MOBA_KEY_LANES - MOBA_HEAD_DIM)))
        w_nat = jnp.concatenate([w[:, :r4], w_mk.reshape(d, MOBA_HEADS * MOBA_KEY_LANES)], axis=1).astype(BF16)
        w_tr = jnp.concatenate([w[:, r4:r4 + MOBA_WIDTH], w[:, r4 + 2 * MOBA_WIDTH:]], axis=1).T.astype(BF16)
        rqkv, rg, mk, mqt, mvt = _inproj(h, w_nat, w_tr)
        ret = _retention(rqkv, rg, ret_gn_gain[layer][None, :], ret_tables)
        moba_lo, moba_hi = _moba(mqt, mk, mvt, slopes)
        wo = w_out[layer].astype(BF16)
        h = _mix_ffn(h, ret, moba_lo, moba_hi, wo[:RET_WIDTH], wo[RET_WIDTH:],
                     ln1_g[layer][None, :], ln1_b[layer][None, :],
                     w_gate[layer].astype(BF16), w_up[layer].astype(BF16), w_down[layer].astype(BF16),
                     ln2_g[layer][None, :], ln2_b[layer][None, :], alpha)
    return h
```

```python
import functools
import math

import numpy as np
import jax
import jax.numpy as jnp
from jax import lax
from jax.experimental import pallas as pl
from jax.experimental.pallas import tpu as pltpu

F32 = jnp.float32
BF16 = jnp.bfloat16

V7X_VMEM_BYTES = 64 * 1024 * 1024
LANES = 128

RET_HEADS = 4
RET_HEAD_DIM = 128
RET_WIDTH = RET_HEADS * RET_HEAD_DIM
RET_CHUNK = 256
RET_CHUNKS_PER_STEP = 4
MOBA_HEADS = 8
MOBA_HEAD_DIM = 64
MOBA_WIDTH = MOBA_HEADS * MOBA_HEAD_DIM
MOBA_BLOCK = 256
MOBA_TOP_K = 3
MOBA_GROUP = 2
MOBA_LOOKAHEAD = 4
MOBA_SUM_ROWS = 16
MOBA_KEY_LANES = LANES
MOBA_WINDOW = 8
MOBA_SPLIT = 3
MOBA_AUX_ROWS = 8 + MOBA_SPLIT * MOBA_WINDOW
LOG2E = 1.4426950408889634
NORM_EPS = 1e-5
NEG_INF = -1e30

TOKEN_TILE = 512
FFN_TOKEN_TILE = 1024
FFN_SUBTILES = 4


def _vmem_limit(estimate_bytes):
    return int(min(V7X_VMEM_BYTES - 4 * 1024 * 1024, max(estimate_bytes, 16 * 1024 * 1024)))


def _nt_dot(a, b):
    return lax.dot_general(a, b, (((1,), (1,)), ((), ())), preferred_element_type=F32)


def _layer_norm(v, g, b):
    mu = jnp.mean(v, axis=-1, keepdims=True)
    d = v - mu
    var = jnp.mean(d * d, axis=-1, keepdims=True)
    return d * lax.rsqrt(var + NORM_EPS) * g + b


def _inproj_kernel(x_ref, wn_ref, wt_ref, rqkv_ref, rg_ref, mk_ref, mqt_ref, mvt_ref):
    x = x_ref[0].astype(BF16)
    nat = jnp.dot(x, wn_ref[...], preferred_element_type=F32)
    rqkv_ref[0] = nat[:, :3 * RET_WIDTH].astype(BF16)
    rg_ref[0] = nat[:, 3 * RET_WIDTH:4 * RET_WIDTH]
    mk_ref[0] = nat[:, 4 * RET_WIDTH:].astype(BF16)
    tr = _nt_dot(wt_ref[...], x)
    for c in range(x.shape[0] // MOBA_BLOCK):
        cols = slice(c * MOBA_BLOCK, (c + 1) * MOBA_BLOCK)
        mqt_ref[0, c] = (tr[:MOBA_WIDTH, cols] * (MOBA_HEAD_DIM ** -0.5 * LOG2E)).astype(BF16)
        mvt_ref[0, c] = tr[MOBA_WIDTH:, cols].astype(BF16)


def _inproj(x, w_nat, w_tr):
    bsz, s_len, d = x.shape
    tm = TOKEN_TILE
    nb = s_len // MOBA_BLOCK
    n_nat = w_nat.shape[1]
    est = (2 * tm * d * 4 + 2 * (w_nat.size + w_tr.size) * 2
           + 2 * tm * (3 * RET_WIDTH * 2 + RET_WIDTH * 4 + 3 * MOBA_WIDTH * 2)
           + tm * (n_nat + 2 * MOBA_WIDTH) * 4 * 2)
    return pl.pallas_call(
        _inproj_kernel,
        grid=(bsz, s_len // tm),
        in_specs=[
            pl.BlockSpec((1, tm, d), lambda b, t: (b, t, 0)),
            pl.BlockSpec((d, n_nat), lambda b, t: (0, 0)),
            pl.BlockSpec((2 * MOBA_WIDTH, d), lambda b, t: (0, 0)),
        ],
        out_specs=[
            pl.BlockSpec((1, tm, 3 * RET_WIDTH), lambda b, t: (b, t, 0)),
            pl.BlockSpec((1, tm, RET_WIDTH), lambda b, t: (b, t, 0)),
            pl.BlockSpec((1, tm, MOBA_WIDTH), lambda b, t: (b, t, 0)),
            pl.BlockSpec((1, tm // MOBA_BLOCK, MOBA_WIDTH, MOBA_BLOCK), lambda b, t: (b, t, 0, 0)),
            pl.BlockSpec((1, tm // MOBA_BLOCK, MOBA_WIDTH, MOBA_BLOCK), lambda b, t: (b, t, 0, 0)),
        ],
        out_shape=[
            jax.ShapeDtypeStruct((bsz, s_len, 3 * RET_WIDTH), BF16),
            jax.ShapeDtypeStruct((bsz, s_len, RET_WIDTH), F32),
            jax.ShapeDtypeStruct((bsz, s_len, MOBA_WIDTH), BF16),
            jax.ShapeDtypeStruct((bsz, nb, MOBA_WIDTH, MOBA_BLOCK), BF16),
            jax.ShapeDtypeStruct((bsz, nb, MOBA_WIDTH, MOBA_BLOCK), BF16),
        ],
        compiler_params=pltpu.CompilerParams(
            dimension_semantics=("parallel", "parallel"),
            vmem_limit_bytes=_vmem_limit(est)),
        name="inproj",
    )(x, w_nat, w_tr)


def _retention_tables():
    h = np.arange(RET_HEADS, dtype=np.float64)
    log_g = np.log1p(-np.exp2(-5.0 - h))
    pos = np.arange(RET_CHUNK, dtype=np.float64)
    diff = pos[:, None] - pos[None, :]
    scale = RET_HEAD_DIM ** -0.5
    intra = np.where(diff >= 0.0, np.exp(log_g[:, None, None] * np.maximum(diff, 0.0)), 0.0) * scale
    k_to_end = np.exp(log_g[:, None] * (RET_CHUNK - 1.0 - pos)[None, :]) * scale
    q_from_start = np.exp(log_g[:, None] * (pos + 1.0)[None, :])
    chunk_decay = np.exp(log_g * RET_CHUNK)
    bcast = lambda t: np.broadcast_to(t[:, :, None], (RET_HEADS, RET_CHUNK, RET_HEAD_DIM))
    return (jnp.asarray(intra, F32), jnp.asarray(bcast(k_to_end), F32),
            jnp.asarray(bcast(q_from_start), F32), jnp.asarray(chunk_decay, F32))


def _retention_kernel(cd_ref, qkv_ref, rg_ref, gain_ref, dec_ref, kte_ref, qfs_ref, o_ref, state_ref):
    e, c = RET_HEAD_DIM, RET_CHUNK

    @pl.when(pl.program_id(1) == 0)
    def _():
        state_ref[...] = jnp.zeros_like(state_ref)

    heads = range(RET_HEADS)
    cols = lambda part, h: slice((part * RET_HEADS + h) * e, (part * RET_HEADS + h + 1) * e)
    state = [state_ref[h] for h in heads]
    for j in range(RET_CHUNKS_PER_STEP):
        rows = slice(j * c, (j + 1) * c)
        q = [qkv_ref[0, rows, cols(0, h)] for h in heads]
        k = [qkv_ref[0, rows, cols(1, h)] for h in heads]
        v = [qkv_ref[0, rows, cols(2, h)] for h in heads]
        raw = [_nt_dot(q[h], k[h]) for h in heads]
        cross = [jnp.dot(q[h], state[h].astype(BF16), preferred_element_type=F32) for h in heads]
        scores = [(raw[h] * dec_ref[h]).astype(BF16) for h in heads]
        k_dec = [(k[h].astype(F32) * kte_ref[h]).T.astype(BF16) for h in heads]
        intra = [jnp.dot(scores[h], v[h], preferred_element_type=F32) for h in heads]
        kv = [jnp.dot(k_dec[h], v[h], preferred_element_type=F32) for h in heads]
        state = [state[h] * cd_ref[h] + kv[h] for h in heads]
        for h in heads:
            y = intra[h] + cross[h] * qfs_ref[h]
            mu = jnp.mean(y, axis=-1, keepdims=True)
            d = y - mu
            var = jnp.mean(d * d, axis=-1, keepdims=True)
            yn = d * lax.rsqrt(var + NORM_EPS) * gain_ref[:, h * e:(h + 1) * e]
            g = rg_ref[0, rows, h * e:(h + 1) * e]
            o_ref[0, rows, h * e:(h + 1) * e] = (g * jax.nn.sigmoid(g) * yn).astype(o_ref.dtype)
    for h in heads:
        state_ref[h] = state[h]


def _retention(rqkv, rg, gain, tables):
    bsz, s_len, _ = rqkv.shape
    dec, kte, qfs, cd = tables
    c, e = RET_CHUNK * RET_CHUNKS_PER_STEP, RET_HEAD_DIM
    assert s_len % c == 0
    full = lambda a: pl.BlockSpec(a.shape, lambda b, n: (0,) * a.ndim)
    return pl.pallas_call(
        _retention_kernel,
        grid=(bsz, s_len // c),
        in_specs=[
            pl.BlockSpec(memory_space=pltpu.SMEM),
            pl.BlockSpec((1, c, 3 * RET_WIDTH), lambda b, n: (b, n, 0)),
            pl.BlockSpec((1, c, RET_WIDTH), lambda b, n: (b, n, 0)),
            full(gain), full(dec), full(kte), full(qfs),
        ],
        out_specs=pl.BlockSpec((1, c, RET_WIDTH), lambda b, n: (b, n, 0)),
        out_shape=jax.ShapeDtypeStruct((bsz, s_len, RET_WIDTH), BF16),
        scratch_shapes=[pltpu.VMEM((RET_HEADS, e, e), F32)],
        compiler_params=pltpu.CompilerParams(
            dimension_semantics=("parallel", "arbitrary")),
        name="retention",
    )(cd, rqkv, rg, gain, dec, kte, qfs)


def _alibi_slopes():
    slopes = np.exp2(-8.0 * (np.arange(MOBA_HEADS, dtype=np.float64) + 1.0) / MOBA_HEADS)
    return jnp.asarray(slopes, F32)


def _moba_past_tiles(n_b):
    ceil_div = lambda a: -(-a // MOBA_GROUP)
    totals = {ceil_div(s) + ceil_div(n_b - 1 - s) for s in range(n_b // 2)}
    assert len(totals) == 1 and n_b % 2 == 0
    return totals.pop()


def _key_side_pattern():
    sp, win = MOBA_SPLIT, MOBA_WINDOW
    assert 2 * sp <= 8 and MOBA_AUX_ROWS <= MOBA_KEY_LANES
    pos = np.arange(win * MOBA_BLOCK)
    pat = np.zeros((pos.size, MOBA_KEY_LANES), np.float32)
    pat[:, :sp] = (pos % MOBA_BLOCK)[:, None]
    pat[:, sp:2 * sp] = 1.0
    for s in range(sp):
        for c in range(win):
            pat[:, 8 + s * win + c] = pos // MOBA_BLOCK == c
    return jnp.asarray(pat, BF16)


def _moba_kernel(slope_ref, qa_ref, qb_ref, k_ref, kpat_ref, vt_ref, oa_ref, ob_ref,
                 kmean_ref, qtop_ref, aux_ref, m_ref, acc_ref):
    blk, hd, grp = MOBA_BLOCK, MOBA_HEAD_DIM, MOBA_GROUP
    kl, aux_rows, win, sp = MOBA_KEY_LANES, MOBA_AUX_ROWS, MOBA_WINDOW, MOBA_SPLIT
    assert k_ref.shape[2] == 2 * hd == kl
    tile = grp * blk
    n_b = vt_ref.shape[1]
    n_past = _moba_past_tiles(n_b)
    pair = pl.program_id(1)
    step = pl.program_id(2)
    own = (step, n_b - 1 - step)
    n_past_first = (step + grp - 1) // grp
    q_refs, o_refs = (qa_ref, qb_ref), (oa_ref, ob_ref)

    @pl.when(step == 0)
    def _():
        for b in range(n_b):
            kb = k_ref[0, b * blk:(b + 1) * blk, :].astype(F32)
            kmean_ref[b:b + 1, :] = jnp.sum(kb, axis=0, keepdims=True) * (1.0 / blk)

    cand_blocks = (n_b // 2, n_b)
    assert cand_blocks[0] % win == 0
    causal = (lax.broadcasted_iota(jnp.int32, (blk, blk), 0)
              <= lax.broadcasted_iota(jnp.int32, (blk, blk), 1))
    row8 = lax.broadcasted_iota(jnp.int32, (8, blk), 0)
    r_q8 = lax.broadcasted_iota(jnp.int32, (8, blk), 1).astype(F32)
    zero_rows = lambda n: jnp.zeros((n, blk), BF16)
    ones_rows = lambda n: (lax.broadcasted_iota(jnp.int32, (MOBA_SUM_ROWS, n), 0) == 0).astype(BF16)

    def bf16_pieces(v):
        pieces = []
        for _ in range(sp):
            piece = v.astype(BF16).astype(F32)
            pieces.append(piece)
            v = v - piece
        return pieces

    def col_max(u):
        return jnp.max(jnp.max(u.reshape(-1, blk, blk), axis=0), axis=0, keepdims=True)

    alibi_rows = []
    for hl in range(2):
        c = slope_ref[2 * pair + hl] * LOG2E
        pieces = bf16_pieces(jnp.where(row8 < sp, c, -c * r_q8))
        rows = jnp.zeros((8, blk), F32)
        for s in range(sp):
            rows = jnp.where((row8 == s) | (row8 == sp + s), pieces[s], rows)
        alibi_rows.append(rows)

    chains = [(o, hl) for o in range(2) for hl in range(2)]
    head_row = lax.broadcasted_iota(jnp.int32, (kl, blk), 0) // hd
    own_scores, gates = {}, {}
    for o, hl in chains:
        q_pair = q_refs[o][0, 0]
        q_head = jnp.where(head_row == hl, q_pair, jnp.zeros_like(q_pair))
        qtop_ref[o, hl] = q_head
        aux_own = jnp.concatenate([alibi_rows[hl], jnp.zeros((aux_rows - 8, blk), F32)], axis=0)
        q_own = jnp.concatenate([q_head, aux_own.astype(BF16), zero_rows(kl - aux_rows)], axis=0)
        kd = jnp.concatenate(
            [k_ref[0, pl.ds(pl.multiple_of(own[o] * blk, blk), blk), :],
             kpat_ref[pl.ds(pl.multiple_of((own[o] % win) * blk, blk), blk), :]], axis=1)
        own_scores[o, hl] = jnp.dot(kd, q_own, preferred_element_type=F32)
        kmean = kmean_ref[:cand_blocks[o], :].astype(BF16)
        gates[o, hl] = jnp.dot(kmean, q_head, preferred_element_type=F32)

    for o, hl in chains:
        u = jnp.where(causal, own_scores[o, hl], NEG_INF)
        m = col_max(u)
        p = jnp.exp2(u - m)
        m_ref[o, hl] = m
        vt = jnp.concatenate([vt_ref[0, own[o], hl * hd:(hl + 1) * hd, :], ones_rows(blk)], axis=0)
        acc_ref[o, hl] = jnp.dot(vt, p.astype(BF16), preferred_element_type=F32)

        jidx = lax.broadcasted_iota(jnp.int32, (cand_blocks[o], blk), 0)
        past = jidx < own[o]
        gate = gates[o, hl]
        taken = jnp.zeros(jidx.shape, jnp.bool_)
        for _ in range(MOBA_TOP_K):
            cand = past & jnp.logical_not(taken)
            best = jnp.max(jnp.where(cand, gate, -jnp.inf), axis=0, keepdims=True)
            pick = cand & (gate == best)
            first = jnp.min(jnp.where(pick, jidx, n_b), axis=0, keepdims=True)
            taken = taken | (jidx == first)
        c = slope_ref[2 * pair + hl] * LOG2E
        block_off = (own[o] - jidx).astype(F32) * (-float(blk) * c)
        bias_pieces = bf16_pieces(jnp.where(taken, block_off, NEG_INF))
        for w in range(cand_blocks[o] // win):
            rows = [alibi_rows[hl]] + [piece[w * win:(w + 1) * win, :] for piece in bias_pieces]
            aux_ref[o, hl, w] = jnp.concatenate(rows, axis=0).astype(BF16)

    tiles = []
    for k in range(n_past):
        which = jnp.where(k >= n_past_first, 1, 0)
        tiles.append((which, k - which * n_past_first))

    def unit_scores(k, hl):
        which, t = tiles[k]
        q_op = jnp.concatenate([qtop_ref[which, hl], aux_ref[which, hl, (t * grp) // win],
                                zero_rows(kl - aux_rows)], axis=0)
        k_rows = jnp.concatenate(
            [k_ref[0, pl.ds(pl.multiple_of(t * tile, tile), tile), :],
             kpat_ref[pl.ds(pl.multiple_of((t % (win // grp)) * tile, tile), tile), :]], axis=1)
        s = jnp.dot(k_rows, q_op, preferred_element_type=F32)
        return s, col_max(s)

    def unit_update(k, hl, s, cmax):
        which, t = tiles[k]
        m = m_ref[which, hl]
        m_new = jnp.maximum(m, cmax)
        alpha = jnp.exp2(m - m_new)
        p = jnp.exp2(s - m_new)
        m_ref[which, hl] = m_new
        vt = jnp.concatenate(
            [vt_ref[0, t * grp + c, hl * hd:(hl + 1) * hd, :] for c in range(grp)], axis=1)
        vt = jnp.concatenate([vt, ones_rows(tile)], axis=0)
        acc_ref[which, hl] = alpha * acc_ref[which, hl] + jnp.dot(
            vt, p.astype(BF16), preferred_element_type=F32)

    units = [(k, hl) for k in range(n_past) for hl in range(2)]
    pending = []
    for idx in range(len(units) + MOBA_LOOKAHEAD):
        if idx < len(units):
            pending.append(unit_scores(*units[idx]))
        if idx >= MOBA_LOOKAHEAD:
            unit_update(*units[idx - MOBA_LOOKAHEAD], *pending.pop(0))

    for o in range(2):
        outs = [acc_ref[o, hl, :hd, :] * (1.0 / acc_ref[o, hl, hd:hd + 1, :]) for hl in range(2)]
        o_refs[o][0] = jnp.concatenate(outs, axis=0).T.astype(oa_ref.dtype)


def _moba(mqt, mk, mvt, slopes):
    bsz, n_b, width, blk = mqt.shape
    s_len = mk.shape[1]
    pw = 2 * MOBA_HEAD_DIM
    tile = MOBA_GROUP * blk
    half = n_b // 2
    kpat = _key_side_pattern()
    est = 2 * s_len * 2 * pw * 2 + 2 * kpat.size * 2 + 4 * 2 * 4 * tile * blk * 4
    out_half = jax.ShapeDtypeStruct((bsz, half * blk, width), BF16)
    return pl.pallas_call(
        _moba_kernel,
        grid=(bsz, width // pw, half),
        in_specs=[
            pl.BlockSpec(memory_space=pltpu.SMEM),
            pl.BlockSpec((1, 1, pw, blk), lambda b, p, s: (b, s, p, 0)),
            pl.BlockSpec((1, 1, pw, blk), lambda b, p, s: (b, n_b - 1 - s, p, 0)),
            pl.BlockSpec((1, s_len, pw), lambda b, p, s: (b, 0, p)),
            pl.BlockSpec(kpat.shape, lambda b, p, s: (0, 0)),
            pl.BlockSpec((1, n_b, pw, blk), lambda b, p, s: (b, 0, p, 0)),
        ],
        out_specs=[pl.BlockSpec((1, blk, pw), lambda b, p, s: (b, s, p)),
                   pl.BlockSpec((1, blk, pw), lambda b, p, s: (b, half - 1 - s, p))],
        out_shape=[out_half, out_half],
        scratch_shapes=[pltpu.VMEM((n_b, pw), F32),
                        pltpu.VMEM((2, 2, pw, blk), BF16),
                        pltpu.VMEM((2, 2, n_b // MOBA_WINDOW, MOBA_AUX_ROWS, blk), BF16),
                        pltpu.VMEM((2, 2, 1, blk), F32),
                        pltpu.VMEM((2, 2, MOBA_HEAD_DIM + MOBA_SUM_ROWS, blk), F32)],
        compiler_params=pltpu.CompilerParams(
            dimension_semantics=("parallel", "parallel", "arbitrary"),
            vmem_limit_bytes=_vmem_limit(est)),
        name="moba",
    )(slopes, mqt, mqt, mk, kpat, mvt)


def _mix_ffn_kernel(alpha, lo_tiles, x_ref, ret_ref, mlo_ref, mhi_ref, wr_ref, wm_ref, g1_ref, b1_ref,
                    wg_ref, wu_ref, wd_ref, g2_ref, b2_ref, o_ref):
    moba = jnp.where(pl.program_id(1) < lo_tiles, mlo_ref[0], mhi_ref[0])
    dot = functools.partial(jnp.dot, preferred_element_type=F32)
    rows = x_ref.shape[1] // FFN_SUBTILES
    part = lambda a, i: a[i * rows:(i + 1) * rows]
    mixed = [dot(part(ret_ref[0], i), wr_ref[...]) + dot(part(moba, i), wm_ref[...])
             for i in range(FFN_SUBTILES)]
    h, act, ffn = {}, {}, {}
    for i in range(FFN_SUBTILES + 2):
        if i < FFN_SUBTILES:
            h[i] = _layer_norm(alpha * part(x_ref[0], i) + mixed[i], g1_ref[...], b1_ref[...])
            hb = h[i].astype(BF16)
            gate = dot(hb, wg_ref[...])
            up = dot(hb, wu_ref[...])
            act[i] = (gate * jax.nn.sigmoid(gate) * up).astype(BF16)
        if 1 <= i <= FFN_SUBTILES:
            ffn[i - 1] = dot(act[i - 1], wd_ref[...])
        if i >= 2:
            j = i - 2
            o_ref[0, j * rows:(j + 1) * rows, :] = _layer_norm(alpha * h[j] + ffn[j], g2_ref[...], b2_ref[...])


def _mix_ffn(x, ret, moba_lo, moba_hi, w_ret, w_moba, g1, b1, w_gate, w_up, w_down, g2, b2, alpha):
    bsz, s_len, d = x.shape
    d_ff = w_gate.shape[1]
    tm = FFN_TOKEN_TILE
    sub = tm // FFN_SUBTILES
    lo_tiles = moba_lo.shape[1] // tm
    assert moba_lo.shape[1] % tm == 0 and moba_hi.shape[1] == s_len - moba_lo.shape[1]
    row = lambda w: pl.BlockSpec((1, tm, w), lambda b_, t: (b_, t, 0))
    lo_row = pl.BlockSpec((1, tm, MOBA_WIDTH), lambda b_, t: (b_, jnp.minimum(t, lo_tiles - 1), 0))
    hi_row = pl.BlockSpec((1, tm, MOBA_WIDTH), lambda b_, t: (b_, jnp.maximum(t - lo_tiles, 0), 0))
    full = lambda a: pl.BlockSpec(a.shape, lambda b_, t: (0, 0), pipeline_mode=pl.Buffered(1))
    est = ((3 * d * d_ff + d * d) * 2 + 2 * 2 * tm * d * 4 + 2 * 3 * tm * RET_WIDTH * 2
           + 2 * sub * d_ff * (4 + 4 + 2) + FFN_SUBTILES * sub * d * 4 * 2)
    return pl.pallas_call(
        functools.partial(_mix_ffn_kernel, alpha, lo_tiles),
        grid=(bsz, s_len // tm),
        in_specs=[row(d), row(RET_WIDTH), lo_row, hi_row, full(w_ret), full(w_moba), full(g1), full(b1),
                  full(w_gate), full(w_up), full(w_down), full(g2), full(b2)],
        out_specs=row(d),
        out_shape=jax.ShapeDtypeStruct((bsz, s_len, d), F32),
        compiler_params=pltpu.CompilerParams(
            dimension_semantics=("parallel", "parallel"),
            vmem_limit_bytes=_vmem_limit(est)),
        name="mix_ffn",
    )(x, ret, moba_lo, moba_hi, w_ret, w_moba, g1, b1, w_gate, w_up, w_down, g2, b2)


def kernel(x, w_in, ret_gn_gain, w_out, ln1_g, ln1_b, w_gate, w_up, w_down, ln2_g, ln2_b):
    bsz, s_len, d = x.shape
    depth = w_in.shape[0]
    assert s_len % math.lcm(RET_CHUNK, MOBA_BLOCK, TOKEN_TILE, FFN_TOKEN_TILE) == 0
    assert w_in.shape[2] == 4 * RET_WIDTH + 3 * MOBA_WIDTH
    alpha = (2 * depth) ** 0.25
    ret_tables = _retention_tables()
    slopes = _alibi_slopes()
    r4 = 4 * RET_WIDTH
    h = x
    for layer in range(depth):
        w = w_in[layer]
        w_nat = jnp.concatenate([w[:, :r4], w[:, r4 + MOBA_WIDTH:r4 + 2 * MOBA_WIDTH]], axis=1).astype(BF16)
        w_tr = jnp.concatenate([w[:, r4:r4 + MOBA_WIDTH], w[:, r4 + 2 * MOBA_WIDTH:]], axis=1).T.astype(BF16)
        rqkv, rg, mk, mqt, mvt = _inproj(h, w_nat, w_tr)
        ret = _retention(rqkv, rg, ret_gn_gain[layer][None, :], ret_tables)
        moba_lo, moba_hi = _moba(mqt, mk, mvt, slopes)
        wo = w_out[layer].astype(BF16)
        h = _mix_ffn(h, ret, moba_lo, moba_hi, wo[:RET_WIDTH], wo[RET_WIDTH:],
                     ln1_g[layer][None, :], ln1_b[layer][None, :],
                     w_gate[layer].astype(BF16), w_up[layer].astype(BF16), w_down[layer].astype(BF16),
                     ln2_g[layer][None, :], ln2_b[layer][None, :], alpha)
    return h
```

```python
import functools
import math

import numpy as np
import jax
import jax.numpy as jnp
from jax import lax
from jax.experimental import pallas as pl
from jax.experimental.pallas import tpu as pltpu

F32 = jnp.float32
BF16 = jnp.bfloat16

V7X_VMEM_BYTES = 64 * 1024 * 1024
LANES = 128

RET_HEADS = 4
RET_HEAD_DIM = 128
RET_WIDTH = RET_HEADS * RET_HEAD_DIM
RET_CHUNK = 256
RET_CHUNKS_PER_STEP = 4
MOBA_HEADS = 8
MOBA_HEAD_DIM = 64
MOBA_WIDTH = MOBA_HEADS * MOBA_HEAD_DIM
MOBA_BLOCK = 256
MOBA_TOP_K = 3
MOBA_GROUP = 1
MOBA_LOOKAHEAD = 5
MOBA_SUM_ROWS = 16
MOBA_KEY_LANES = LANES
MOBA_WINDOW = 8
MOBA_SPLIT = 3
MOBA_AUX_ROWS = 8 + MOBA_SPLIT * MOBA_WINDOW
LOG2E = 1.4426950408889634
NORM_EPS = 1e-5
NEG_INF = -1e30

TOKEN_TILE = 512
FFN_TOKEN_TILE = 1024
FFN_SUBTILES = 4


def _vmem_limit(estimate_bytes):
    return int(min(V7X_VMEM_BYTES - 4 * 1024 * 1024, max(estimate_bytes, 16 * 1024 * 1024)))


def _nt_dot(a, b):
    return lax.dot_general(a, b, (((1,), (1,)), ((), ())), preferred_element_type=F32)


def _layer_norm(v, g, b):
    mu = jnp.mean(v, axis=-1, keepdims=True)
    d = v - mu
    var = jnp.mean(d * d, axis=-1, keepdims=True)
    return d * lax.rsqrt(var + NORM_EPS) * g + b


def _inproj_kernel(x_ref, wn_ref, wt_ref, rqkv_ref, rg_ref, mk_ref, mqt_ref, mvt_ref):
    x = x_ref[0].astype(BF16)
    nat = jnp.dot(x, wn_ref[...], preferred_element_type=F32)
    rqkv_ref[0] = nat[:, :3 * RET_WIDTH].astype(BF16)
    rg_ref[0] = nat[:, 3 * RET_WIDTH:4 * RET_WIDTH]
    mk_ref[0] = nat[:, 4 * RET_WIDTH:].astype(BF16)
    tr = _nt_dot(wt_ref[...], x)
    for c in range(x.shape[0] // MOBA_BLOCK):
        cols = slice(c * MOBA_BLOCK, (c + 1) * MOBA_BLOCK)
        mqt_ref[0, c] = (tr[:MOBA_WIDTH, cols] * (MOBA_HEAD_DIM ** -0.5 * LOG2E)).astype(BF16)
        mvt_ref[0, c] = tr[MOBA_WIDTH:, cols].astype(BF16)


def _inproj(x, w_nat, w_tr):
    bsz, s_len, d = x.shape
    tm = TOKEN_TILE
    nb = s_len // MOBA_BLOCK
    n_nat = w_nat.shape[1]
    est = (2 * tm * d * 4 + 2 * (w_nat.size + w_tr.size) * 2
           + 2 * tm * (3 * RET_WIDTH * 2 + RET_WIDTH * 4 + 3 * MOBA_WIDTH * 2)
           + tm * (n_nat + 2 * MOBA_WIDTH) * 4 * 2)
    return pl.pallas_call(
        _inproj_kernel,
        grid=(bsz, s_len // tm),
        in_specs=[
            pl.BlockSpec((1, tm, d), lambda b, t: (b, t, 0)),
            pl.BlockSpec((d, n_nat), lambda b, t: (0, 0)),
            pl.BlockSpec((2 * MOBA_WIDTH, d), lambda b, t: (0, 0)),
        ],
        out_specs=[
            pl.BlockSpec((1, tm, 3 * RET_WIDTH), lambda b, t: (b, t, 0)),
            pl.BlockSpec((1, tm, RET_WIDTH), lambda b, t: (b, t, 0)),
            pl.BlockSpec((1, tm, MOBA_WIDTH), lambda b, t: (b, t, 0)),
            pl.BlockSpec((1, tm // MOBA_BLOCK, MOBA_WIDTH, MOBA_BLOCK), lambda b, t: (b, t, 0, 0)),
            pl.BlockSpec((1, tm // MOBA_BLOCK, MOBA_WIDTH, MOBA_BLOCK), lambda b, t: (b, t, 0, 0)),
        ],
        out_shape=[
            jax.ShapeDtypeStruct((bsz, s_len, 3 * RET_WIDTH), BF16),
            jax.ShapeDtypeStruct((bsz, s_len, RET_WIDTH), F32),
            jax.ShapeDtypeStruct((bsz, s_len, MOBA_WIDTH), BF16),
            jax.ShapeDtypeStruct((bsz, nb, MOBA_WIDTH, MOBA_BLOCK), BF16),
            jax.ShapeDtypeStruct((bsz, nb, MOBA_WIDTH, MOBA_BLOCK), BF16),
        ],
        compiler_params=pltpu.CompilerParams(
            dimension_semantics=("parallel", "parallel"),
            vmem_limit_bytes=_vmem_limit(est)),
        name="inproj",
    )(x, w_nat, w_tr)


def _retention_tables():
    h = np.arange(RET_HEADS, dtype=np.float64)
    log_g = np.log1p(-np.exp2(-5.0 - h))
    pos = np.arange(RET_CHUNK, dtype=np.float64)
    diff = pos[:, None] - pos[None, :]
    scale = RET_HEAD_DIM ** -0.5
    intra = np.where(diff >= 0.0, np.exp(log_g[:, None, None] * np.maximum(diff, 0.0)), 0.0) * scale
    k_to_end = np.exp(log_g[:, None] * (RET_CHUNK - 1.0 - pos)[None, :]) * scale
    q_from_start = np.exp(log_g[:, None] * (pos + 1.0)[None, :])
    chunk_decay = np.exp(log_g * RET_CHUNK)
    bcast = lambda t: np.broadcast_to(t[:, :, None], (RET_HEADS, RET_CHUNK, RET_HEAD_DIM))
    return (jnp.asarray(intra, F32), jnp.asarray(bcast(k_to_end), F32),
            jnp.asarray(bcast(q_from_start), F32), jnp.asarray(chunk_decay, F32))


def _retention_kernel(cd_ref, qkv_ref, rg_ref, gain_ref, dec_ref, kte_ref, qfs_ref, o_ref, state_ref):
    e, c = RET_HEAD_DIM, RET_CHUNK

    @pl.when(pl.program_id(1) == 0)
    def _():
        state_ref[...] = jnp.zeros_like(state_ref)

    heads = range(RET_HEADS)
    cols = lambda part, h: slice((part * RET_HEADS + h) * e, (part * RET_HEADS + h + 1) * e)
    state = [state_ref[h] for h in heads]
    for j in range(RET_CHUNKS_PER_STEP):
        rows = slice(j * c, (j + 1) * c)
        q = [qkv_ref[0, rows, cols(0, h)] for h in heads]
        k = [qkv_ref[0, rows, cols(1, h)] for h in heads]
        v = [qkv_ref[0, rows, cols(2, h)] for h in heads]
        raw = [_nt_dot(q[h], k[h]) for h in heads]
        cross = [jnp.dot(q[h], state[h].astype(BF16), preferred_element_type=F32) for h in heads]
        scores = [(raw[h] * dec_ref[h]).astype(BF16) for h in heads]
        k_dec = [(k[h].astype(F32) * kte_ref[h]).T.astype(BF16) for h in heads]
        intra = [jnp.dot(scores[h], v[h], preferred_element_type=F32) for h in heads]
        kv = [jnp.dot(k_dec[h], v[h], preferred_element_type=F32) for h in heads]
        state = [state[h] * cd_ref[h] + kv[h] for h in heads]
        for h in heads:
            y = intra[h] + cross[h] * qfs_ref[h]
            mu = jnp.mean(y, axis=-1, keepdims=True)
            d = y - mu
            var = jnp.mean(d * d, axis=-1, keepdims=True)
            yn = d * lax.rsqrt(var + NORM_EPS) * gain_ref[:, h * e:(h + 1) * e]
            g = rg_ref[0, rows, h * e:(h + 1) * e]
            o_ref[0, rows, h * e:(h + 1) * e] = (g * jax.nn.sigmoid(g) * yn).astype(o_ref.dtype)
    for h in heads:
        state_ref[h] = state[h]


def _retention(rqkv, rg, gain, tables):
    bsz, s_len, _ = rqkv.shape
    dec, kte, qfs, cd = tables
    c, e = RET_CHUNK * RET_CHUNKS_PER_STEP, RET_HEAD_DIM
    assert s_len % c == 0
    full = lambda a: pl.BlockSpec(a.shape, lambda b, n: (0,) * a.ndim)
    return pl.pallas_call(
        _retention_kernel,
        grid=(bsz, s_len // c),
        in_specs=[
            pl.BlockSpec(memory_space=pltpu.SMEM),
            pl.BlockSpec((1, c, 3 * RET_WIDTH), lambda b, n: (b, n, 0)),
            pl.BlockSpec((1, c, RET_WIDTH), lambda b, n: (b, n, 0)),
            full(gain), full(dec), full(kte), full(qfs),
        ],
        out_specs=pl.BlockSpec((1, c, RET_WIDTH), lambda b, n: (b, n, 0)),
        out_shape=jax.ShapeDtypeStruct((bsz, s_len, RET_WIDTH), BF16),
        scratch_shapes=[pltpu.VMEM((RET_HEADS, e, e), F32)],
        compiler_params=pltpu.CompilerParams(
            dimension_semantics=("parallel", "arbitrary")),
        name="retention",
    )(cd, rqkv, rg, gain, dec, kte, qfs)


def _alibi_slopes():
    slopes = np.exp2(-8.0 * (np.arange(MOBA_HEADS, dtype=np.float64) + 1.0) / MOBA_HEADS)
    return jnp.asarray(slopes, F32)


def _moba_past_tiles(n_b):
    ceil_div = lambda a: -(-a // MOBA_GROUP)
    totals = {ceil_div(s) + ceil_div(n_b - 1 - s) for s in range(n_b // 2)}
    assert len(totals) == 1 and n_b % 2 == 0
    return totals.pop()


def _key_side_pattern():
    sp, win = MOBA_SPLIT, MOBA_WINDOW
    assert 2 * sp <= 8 and MOBA_AUX_ROWS <= MOBA_KEY_LANES
    pos = np.arange(win * MOBA_BLOCK)
    pat = np.zeros((pos.size, MOBA_KEY_LANES), np.float32)
    pat[:, :sp] = (pos % MOBA_BLOCK)[:, None]
    pat[:, sp:2 * sp] = 1.0
    for s in range(sp):
        for c in range(win):
            pat[:, 8 + s * win + c] = pos // MOBA_BLOCK == c
    return jnp.asarray(pat, BF16)


def _moba_kernel(slope_ref, qa_ref, qb_ref, k_ref, kpat_ref, vt_ref, oa_ref, ob_ref,
                 kmean_ref, qtop_ref, aux_ref, m_ref, acc_ref):
    blk, hd, grp = MOBA_BLOCK, MOBA_HEAD_DIM, MOBA_GROUP
    kl, aux_rows, win, sp = MOBA_KEY_LANES, MOBA_AUX_ROWS, MOBA_WINDOW, MOBA_SPLIT
    assert k_ref.shape[2] == 2 * hd == kl
    tile = grp * blk
    n_b = vt_ref.shape[1]
    n_past = _moba_past_tiles(n_b)
    pair = pl.program_id(1)
    step = pl.program_id(2)
    own = (step, n_b - 1 - step)
    n_past_first = (step + grp - 1) // grp
    q_refs, o_refs = (qa_ref, qb_ref), (oa_ref, ob_ref)

    @pl.when(step == 0)
    def _():
        for b in range(n_b):
            kb = k_ref[0, b * blk:(b + 1) * blk, :].astype(F32)
            kmean_ref[b:b + 1, :] = jnp.sum(kb, axis=0, keepdims=True) * (1.0 / blk)

    cand_blocks = (n_b // 2, n_b)
    assert cand_blocks[0] % win == 0
    causal = (lax.broadcasted_iota(jnp.int32, (blk, blk), 0)
              <= lax.broadcasted_iota(jnp.int32, (blk, blk), 1))
    row8 = lax.broadcasted_iota(jnp.int32, (8, blk), 0)
    r_q8 = lax.broadcasted_iota(jnp.int32, (8, blk), 1).astype(F32)
    zero_rows = lambda n: jnp.zeros((n, blk), BF16)
    ones_rows = lambda n: (lax.broadcasted_iota(jnp.int32, (MOBA_SUM_ROWS, n), 0) == 0).astype(BF16)

    def bf16_pieces(v):
        pieces = []
        for _ in range(sp):
            piece = v.astype(BF16).astype(F32)
            pieces.append(piece)
            v = v - piece
        return pieces

    def col_max(u):
        return jnp.max(jnp.max(u.reshape(-1, blk, blk), axis=0), axis=0, keepdims=True)

    alibi_rows = []
    for hl in range(2):
        c = slope_ref[2 * pair + hl] * LOG2E
        pieces = bf16_pieces(jnp.where(row8 < sp, c, -c * r_q8))
        rows = jnp.zeros((8, blk), F32)
        for s in range(sp):
            rows = jnp.where((row8 == s) | (row8 == sp + s), pieces[s], rows)
        alibi_rows.append(rows)

    chains = [(o, hl) for o in range(2) for hl in range(2)]
    head_row = lax.broadcasted_iota(jnp.int32, (kl, blk), 0) // hd
    own_scores, gates = {}, {}
    for o, hl in chains:
        q_pair = q_refs[o][0, 0]
        q_head = jnp.where(head_row == hl, q_pair, jnp.zeros_like(q_pair))
        qtop_ref[o, hl] = q_head
        aux_own = jnp.concatenate([alibi_rows[hl], jnp.zeros((aux_rows - 8, blk), F32)], axis=0)
        q_own = jnp.concatenate([q_head, aux_own.astype(BF16), zero_rows(kl - aux_rows)], axis=0)
        kd = jnp.concatenate(
            [k_ref[0, pl.ds(pl.multiple_of(own[o] * blk, blk), blk), :],
             kpat_ref[pl.ds(pl.multiple_of((own[o] % win) * blk, blk), blk), :]], axis=1)
        own_scores[o, hl] = jnp.dot(kd, q_own, preferred_element_type=F32)
        kmean = kmean_ref[:cand_blocks[o], :].astype(BF16)
        gates[o, hl] = jnp.dot(kmean, q_head, preferred_element_type=F32)

    for o, hl in chains:
        u = jnp.where(causal, own_scores[o, hl], NEG_INF)
        m = col_max(u)
        p = jnp.exp2(u - m)
        m_ref[o, hl] = m
        vt = jnp.concatenate([vt_ref[0, own[o], hl * hd:(hl + 1) * hd, :], ones_rows(blk)], axis=0)
        acc_ref[o, hl] = jnp.dot(vt, p.astype(BF16), preferred_element_type=F32)

        jidx = lax.broadcasted_iota(jnp.int32, (cand_blocks[o], blk), 0)
        past = jidx < own[o]
        gate = gates[o, hl]
        taken = jnp.zeros(jidx.shape, jnp.bool_)
        for _ in range(MOBA_TOP_K):
            cand = past & jnp.logical_not(taken)
            best = jnp.max(jnp.where(cand, gate, -jnp.inf), axis=0, keepdims=True)
            pick = cand & (gate == best)
            first = jnp.min(jnp.where(pick, jidx, n_b), axis=0, keepdims=True)
            taken = taken | (jidx == first)
        c = slope_ref[2 * pair + hl] * LOG2E
        block_off = (own[o] - jidx).astype(F32) * (-float(blk) * c)
        bias_pieces = bf16_pieces(jnp.where(taken, block_off, NEG_INF))
        for w in range(cand_blocks[o] // win):
            rows = [alibi_rows[hl]] + [piece[w * win:(w + 1) * win, :] for piece in bias_pieces]
            aux_ref[o, hl, w] = jnp.concatenate(rows, axis=0).astype(BF16)

    tiles = []
    for k in range(n_past):
        which = jnp.where(k >= n_past_first, 1, 0)
        tiles.append((which, k - which * n_past_first))

    def unit_scores(k, hl):
        which, t = tiles[k]
        q_op = jnp.concatenate([qtop_ref[which, hl], aux_ref[which, hl, (t * grp) // win],
                                zero_rows(kl - aux_rows)], axis=0)
        k_rows = jnp.concatenate(
            [k_ref[0, pl.ds(pl.multiple_of(t * tile, tile), tile), :],
             kpat_ref[pl.ds(pl.multiple_of((t % (win // grp)) * tile, tile), tile), :]], axis=1)
        s = jnp.dot(k_rows, q_op, preferred_element_type=F32)
        return s, col_max(s)

    def unit_update(k, hl, s, cmax):
        which, t = tiles[k]
        m = m_ref[which, hl]
        m_new = jnp.maximum(m, cmax)
        alpha = jnp.exp2(m - m_new)
        p = jnp.exp2(s - m_new)
        m_ref[which, hl] = m_new
        vt = jnp.concatenate(
            [vt_ref[0, t * grp + c, hl * hd:(hl + 1) * hd, :] for c in range(grp)], axis=1)
        vt = jnp.concatenate([vt, ones_rows(tile)], axis=0)
        acc_ref[which, hl] = alpha * acc_ref[which, hl] + jnp.dot(
            vt, p.astype(BF16), preferred_element_type=F32)

    units = [(k, hl) for k in range(n_past) for hl in range(2)]
    pending = []
    for idx in range(len(units) + MOBA_LOOKAHEAD):
        if idx < len(units):
            pending.append(unit_scores(*units[idx]))
        if idx >= MOBA_LOOKAHEAD:
            unit_update(*units[idx - MOBA_LOOKAHEAD], *pending.pop(0))

    for o in range(2):
        outs = [acc_ref[o, hl, :hd, :] * (1.0 / acc_ref[o, hl, hd:hd + 1, :]) for hl in range(2)]
        o_refs[o][0] = jnp.concatenate(outs, axis=0).T.astype(oa_ref.dtype)


def _moba(mqt, mk, mvt, slopes):
    bsz, n_b, width, blk = mqt.shape
    s_len = mk.shape[1]
    pw = 2 * MOBA_HEAD_DIM
    tile = MOBA_GROUP * blk
    half = n_b // 2
    kpat = _key_side_pattern()
    est = 2 * s_len * 2 * pw * 2 + 2 * kpat.size * 2 + 4 * 2 * 4 * tile * blk * 4
    out_half = jax.ShapeDtypeStruct((bsz, half * blk, width), BF16)
    return pl.pallas_call(
        _moba_kernel,
        grid=(bsz, width // pw, half),
        in_specs=[
            pl.BlockSpec(memory_space=pltpu.SMEM),
            pl.BlockSpec((1, 1, pw, blk), lambda b, p, s: (b, s, p, 0)),
            pl.BlockSpec((1, 1, pw, blk), lambda b, p, s: (b, n_b - 1 - s, p, 0)),
            pl.BlockSpec((1, s_len, pw), lambda b, p, s: (b, 0, p)),
            pl.BlockSpec(kpat.shape, lambda b, p, s: (0, 0)),
            pl.BlockSpec((1, n_b, pw, blk), lambda b, p, s: (b, 0, p, 0)),
        ],
        out_specs=[pl.BlockSpec((1, blk, pw), lambda b, p, s: (b, s, p)),
                   pl.BlockSpec((1, blk, pw), lambda b, p, s: (b, half - 1 - s, p))],
        out_shape=[out_half, out_half],
        scratch_shapes=[pltpu.VMEM((n_b, pw), F32),
                        pltpu.VMEM((2, 2, pw, blk), BF16),
                        pltpu.VMEM((2, 2, n_b // MOBA_WINDOW, MOBA_AUX_ROWS, blk), BF16),
                        pltpu.VMEM((2, 2, 1, blk), F32),
                        pltpu.VMEM((2, 2, MOBA_HEAD_DIM + MOBA_SUM_ROWS, blk), F32)],
        compiler_params=pltpu.CompilerParams(
            dimension_semantics=("parallel", "parallel", "arbitrary"),
            vmem_limit_bytes=_vmem_limit(est)),
        name="moba",
    )(slopes, mqt, mqt, mk, kpat, mvt)


def _mix_ffn_kernel(alpha, lo_tiles, x_ref, ret_ref, mlo_ref, mhi_ref, wr_ref, wm_ref, g1_ref, b1_ref,
                    wg_ref, wu_ref, wd_ref, g2_ref, b2_ref, o_ref):
    moba = jnp.where(pl.program_id(1) < lo_tiles, mlo_ref[0], mhi_ref[0])
    dot = functools.partial(jnp.dot, preferred_element_type=F32)
    rows = x_ref.shape[1] // FFN_SUBTILES
    part = lambda a, i: a[i * rows:(i + 1) * rows]
    mixed = [dot(part(ret_ref[0], i), wr_ref[...]) + dot(part(moba, i), wm_ref[...])
             for i in range(FFN_SUBTILES)]
    h, act, ffn = {}, {}, {}
    for i in range(FFN_SUBTILES + 2):
        if i < FFN_SUBTILES:
            h[i] = _layer_norm(alpha * part(x_ref[0], i) + mixed[i], g1_ref[...], b1_ref[...])
            hb = h[i].astype(BF16)
            gate = dot(hb, wg_ref[...])
            up = dot(hb, wu_ref[...])
            act[i] = (gate * jax.nn.sigmoid(gate) * up).astype(BF16)
        if 1 <= i <= FFN_SUBTILES:
            ffn[i - 1] = dot(act[i - 1], wd_ref[...])
        if i >= 2:
            j = i - 2
            o_ref[0, j * rows:(j + 1) * rows, :] = _layer_norm(alpha * h[j] + ffn[j], g2_ref[...], b2_ref[...])


def _mix_ffn(x, ret, moba_lo, moba_hi, w_ret, w_moba, g1, b1, w_gate, w_up, w_down, g2, b2, alpha):
    bsz, s_len, d = x.shape
    d_ff = w_gate.shape[1]
    tm = FFN_TOKEN_TILE
    sub = tm // FFN_SUBTILES
    lo_tiles = moba_lo.shape[1] // tm
    assert moba_lo.shape[1] % tm == 0 and moba_hi.shape[1] == s_len - moba_lo.shape[1]
    row = lambda w: pl.BlockSpec((1, tm, w), lambda b_, t: (b_, t, 0))
    lo_row = pl.BlockSpec((1, tm, MOBA_WIDTH), lambda b_, t: (b_, jnp.minimum(t, lo_tiles - 1), 0))
    hi_row = pl.BlockSpec((1, tm, MOBA_WIDTH), lambda b_, t: (b_, jnp.maximum(t - lo_tiles, 0), 0))
    full = lambda a: pl.BlockSpec(a.shape, lambda b_, t: (0, 0), pipeline_mode=pl.Buffered(1))
    est = ((3 * d * d_ff + d * d) * 2 + 2 * 2 * tm * d * 4 + 2 * 3 * tm * RET_WIDTH * 2
           + 2 * sub * d_ff * (4 + 4 + 2) + FFN_SUBTILES * sub * d * 4 * 2)
    return pl.pallas_call(
        functools.partial(_mix_ffn_kernel, alpha, lo_tiles),
        grid=(bsz, s_len // tm),
        in_specs=[row(d), row(RET_WIDTH), lo_row, hi_row, full(w_ret), full(w_moba), full(g1), full(b1),
                  full(w_gate), full(w_up), full(w_down), full(g2), full(b2)],
        out_specs=row(d),
        out_shape=jax.ShapeDtypeStruct((bsz, s_len, d), F32),
        compiler_params=pltpu.CompilerParams(
            dimension_semantics=("parallel", "parallel"),
            vmem_limit_bytes=_vmem_limit(est)),
        name="mix_ffn",
    )(x, ret, moba_lo, moba_hi, w_ret, w_moba, g1, b1, w_gate, w_up, w_down, g2, b2)


def kernel(x, w_in, ret_gn_gain, w_out, ln1_g, ln1_b, w_gate, w_up, w_down, ln2_g, ln2_b):
    bsz, s_len, d = x.shape
    depth = w_in.shape[0]
    assert s_len % math.lcm(RET_CHUNK, MOBA_BLOCK, TOKEN_TILE, FFN_TOKEN_TILE) == 0
    assert w_in.shape[2] == 4 * RET_WIDTH + 3 * MOBA_WIDTH
    alpha = (2 * depth) ** 0.25
    ret_tables = _retention_tables()
    slopes = _alibi_slopes()
    r4 = 4 * RET_WIDTH
    h = x
    for layer in range(depth):
        w = w_in[layer]
        w_nat = jnp.concatenate([w[:, :r4], w[:, r4 + MOBA_WIDTH:r4 + 2 * MOBA_WIDTH]], axis=1).astype(BF16)
        w_tr = jnp.concatenate([w[:, r4:r4 + MOBA_WIDTH], w[:, r4 + 2 * MOBA_WIDTH:]], axis=1).T.astype(BF16)
        rqkv, rg, mk, mqt, mvt = _inproj(h, w_nat, w_tr)
        ret = _retention(rqkv, rg, ret_gn_gain[layer][None, :], ret_tables)
        moba_lo, moba_hi = _moba(mqt, mk, mvt, slopes)
        wo = w_out[layer].astype(BF16)
        h = _mix_ffn(h, ret, moba_lo, moba_hi, wo[:RET_WIDTH], wo[RET_WIDTH:],
                     ln1_g[layer][None, :], ln1_b[layer][None, :],
                     w_gate[layer].astype(BF16), w_up[layer].astype(BF16), w_down[layer].astype(BF16),
                     ln2_g[layer][None, :], ln2_b[layer][None, :], alpha)
    return h
```

```python
import functools
import math

import numpy as np
import jax
import jax.numpy as jnp
from jax import lax
from jax.experimental import pallas as pl
from jax.experimental.pallas import tpu as pltpu

F32 = jnp.float32
BF16 = jnp.bfloat16

V7X_VMEM_BYTES = 64 * 1024 * 1024
LANES = 128

RET_HEADS = 4
RET_HEAD_DIM = 128
RET_WIDTH = RET_HEADS * RET_HEAD_DIM
RET_CHUNK = 256
RET_CHUNKS_PER_STEP = 4
MOBA_HEADS = 8
MOBA_HEAD_DIM = 64
MOBA_WIDTH = MOBA_HEADS * MOBA_HEAD_DIM
MOBA_BLOCK = 256
MOBA_TOP_K = 3
MOBA_GROUP = 1
MOBA_LOOKAHEAD = 5
MOBA_SUM_ROWS = 16
MOBA_KEY_LANES = LANES
MOBA_WINDOW = 8
MOBA_SPLIT = 3
MOBA_AUX_ROWS = 8 + MOBA_SPLIT * MOBA_WINDOW
LOG2E = 1.4426950408889634
NORM_EPS = 1e-5
NEG_INF = -1e30

TOKEN_TILE = 1024
FFN_TOKEN_TILE = 1024
FFN_SUBTILES = 4


def _vmem_limit(estimate_bytes):
    return int(min(V7X_VMEM_BYTES - 4 * 1024 * 1024, max(estimate_bytes, 16 * 1024 * 1024)))


def _nt_dot(a, b):
    return lax.dot_general(a, b, (((1,), (1,)), ((), ())), preferred_element_type=F32)


def _layer_norm(v, g, b):
    mu = jnp.mean(v, axis=-1, keepdims=True)
    d = v - mu
    var = jnp.mean(d * d, axis=-1, keepdims=True)
    return d * lax.rsqrt(var + NORM_EPS) * g + b


def _inproj_kernel(x_ref, wn_ref, wt_ref, rqkv_ref, rg_ref, mk_ref, mqt_ref, mvt_ref):
    x = x_ref[0].astype(BF16)
    nat = jnp.dot(x, wn_ref[...], preferred_element_type=F32)
    rqkv_ref[0] = nat[:, :3 * RET_WIDTH].astype(BF16)
    rg_ref[0] = nat[:, 3 * RET_WIDTH:4 * RET_WIDTH]
    mk_ref[0] = nat[:, 4 * RET_WIDTH:].astype(BF16)
    tr = _nt_dot(wt_ref[...], x)
    for c in range(x.shape[0] // MOBA_BLOCK):
        cols = slice(c * MOBA_BLOCK, (c + 1) * MOBA_BLOCK)
        mqt_ref[0, c] = (tr[:MOBA_WIDTH, cols] * (MOBA_HEAD_DIM ** -0.5 * LOG2E)).astype(BF16)
        mvt_ref[0, c] = tr[MOBA_WIDTH:, cols].astype(BF16)


def _inproj(x, w_nat, w_tr):
    bsz, s_len, d = x.shape
    tm = TOKEN_TILE
    nb = s_len // MOBA_BLOCK
    n_nat = w_nat.shape[1]
    est = (2 * tm * d * 4 + (w_nat.size + w_tr.size) * 2
           + 2 * tm * (3 * RET_WIDTH * 2 + RET_WIDTH * 4 + 3 * MOBA_WIDTH * 2)
           + tm * (n_nat + 2 * MOBA_WIDTH) * 4 * 2)
    return pl.pallas_call(
        _inproj_kernel,
        grid=(bsz, s_len // tm),
        in_specs=[
            pl.BlockSpec((1, tm, d), lambda b, t: (b, t, 0)),
            pl.BlockSpec((d, n_nat), lambda b, t: (0, 0), pipeline_mode=pl.Buffered(1)),
            pl.BlockSpec((2 * MOBA_WIDTH, d), lambda b, t: (0, 0), pipeline_mode=pl.Buffered(1)),
        ],
        out_specs=[
            pl.BlockSpec((1, tm, 3 * RET_WIDTH), lambda b, t: (b, t, 0)),
            pl.BlockSpec((1, tm, RET_WIDTH), lambda b, t: (b, t, 0)),
            pl.BlockSpec((1, tm, MOBA_WIDTH), lambda b, t: (b, t, 0)),
            pl.BlockSpec((1, tm // MOBA_BLOCK, MOBA_WIDTH, MOBA_BLOCK), lambda b, t: (b, t, 0, 0)),
            pl.BlockSpec((1, tm // MOBA_BLOCK, MOBA_WIDTH, MOBA_BLOCK), lambda b, t: (b, t, 0, 0)),
        ],
        out_shape=[
            jax.ShapeDtypeStruct((bsz, s_len, 3 * RET_WIDTH), BF16),
            jax.ShapeDtypeStruct((bsz, s_len, RET_WIDTH), F32),
            jax.ShapeDtypeStruct((bsz, s_len, MOBA_WIDTH), BF16),
            jax.ShapeDtypeStruct((bsz, nb, MOBA_WIDTH, MOBA_BLOCK), BF16),
            jax.ShapeDtypeStruct((bsz, nb, MOBA_WIDTH, MOBA_BLOCK), BF16),
        ],
        compiler_params=pltpu.CompilerParams(
            dimension_semantics=("parallel", "parallel"),
            vmem_limit_bytes=_vmem_limit(est)),
        name="inproj",
    )(x, w_nat, w_tr)


def _retention_tables():
    h = np.arange(RET_HEADS, dtype=np.float64)
    log_g = np.log1p(-np.exp2(-5.0 - h))
    pos = np.arange(RET_CHUNK, dtype=np.float64)
    diff = pos[:, None] - pos[None, :]
    scale = RET_HEAD_DIM ** -0.5
    intra = np.where(diff >= 0.0, np.exp(log_g[:, None, None] * np.maximum(diff, 0.0)), 0.0) * scale
    k_to_end = np.exp(log_g[:, None] * (RET_CHUNK - 1.0 - pos)[None, :]) * scale
    q_from_start = np.exp(log_g[:, None] * (pos + 1.0)[None, :])
    chunk_decay = np.exp(log_g * RET_CHUNK)
    bcast = lambda t: np.broadcast_to(t[:, :, None], (RET_HEADS, RET_CHUNK, RET_HEAD_DIM))
    return (jnp.asarray(intra, F32), jnp.asarray(bcast(k_to_end), F32),
            jnp.asarray(bcast(q_from_start), F32), jnp.asarray(chunk_decay, F32))


def _retention_kernel(cd_ref, qkv_ref, rg_ref, gain_ref, dec_ref, kte_ref, qfs_ref, o_ref, state_ref):
    e, c = RET_HEAD_DIM, RET_CHUNK

    @pl.when(pl.program_id(1) == 0)
    def _():
        state_ref[...] = jnp.zeros_like(state_ref)

    heads = range(RET_HEADS)
    cols = lambda part, h: slice((part * RET_HEADS + h) * e, (part * RET_HEADS + h + 1) * e)
    state = [state_ref[h] for h in heads]
    for j in range(RET_CHUNKS_PER_STEP):
        rows = slice(j * c, (j + 1) * c)
        q = [qkv_ref[0, rows, cols(0, h)] for h in heads]
        k = [qkv_ref[0, rows, cols(1, h)] for h in heads]
        v = [qkv_ref[0, rows, cols(2, h)] for h in heads]
        raw = [_nt_dot(q[h], k[h]) for h in heads]
        cross = [jnp.dot(q[h], state[h].astype(BF16), preferred_element_type=F32) for h in heads]
        scores = [(raw[h] * dec_ref[h]).astype(BF16) for h in heads]
        k_dec = [(k[h].astype(F32) * kte_ref[h]).T.astype(BF16) for h in heads]
        intra = [jnp.dot(scores[h], v[h], preferred_element_type=F32) for h in heads]
        kv = [jnp.dot(k_dec[h], v[h], preferred_element_type=F32) for h in heads]
        state = [state[h] * cd_ref[h] + kv[h] for h in heads]
        for h in heads:
            y = intra[h] + cross[h] * qfs_ref[h]
            mu = jnp.mean(y, axis=-1, keepdims=True)
            d = y - mu
            var = jnp.mean(d * d, axis=-1, keepdims=True)
            yn = d * lax.rsqrt(var + NORM_EPS) * gain_ref[:, h * e:(h + 1) * e]
            g = rg_ref[0, rows, h * e:(h + 1) * e]
            o_ref[0, rows, h * e:(h + 1) * e] = (g * jax.nn.sigmoid(g) * yn).astype(o_ref.dtype)
    for h in heads:
        state_ref[h] = state[h]


def _retention(rqkv, rg, gain, tables):
    bsz, s_len, _ = rqkv.shape
    dec, kte, qfs, cd = tables
    c, e = RET_CHUNK * RET_CHUNKS_PER_STEP, RET_HEAD_DIM
    assert s_len % c == 0
    full = lambda a: pl.BlockSpec(a.shape, lambda b, n: (0,) * a.ndim)
    return pl.pallas_call(
        _retention_kernel,
        grid=(bsz, s_len // c),
        in_specs=[
            pl.BlockSpec(memory_space=pltpu.SMEM),
            pl.BlockSpec((1, c, 3 * RET_WIDTH), lambda b, n: (b, n, 0)),
            pl.BlockSpec((1, c, RET_WIDTH), lambda b, n: (b, n, 0)),
            full(gain), full(dec), full(kte), full(qfs),
        ],
        out_specs=pl.BlockSpec((1, c, RET_WIDTH), lambda b, n: (b, n, 0)),
        out_shape=jax.ShapeDtypeStruct((bsz, s_len, RET_WIDTH), BF16),
        scratch_shapes=[pltpu.VMEM((RET_HEADS, e, e), F32)],
        compiler_params=pltpu.CompilerParams(
            dimension_semantics=("parallel", "arbitrary")),
        name="retention",
    )(cd, rqkv, rg, gain, dec, kte, qfs)


def _alibi_slopes():
    slopes = np.exp2(-8.0 * (np.arange(MOBA_HEADS, dtype=np.float64) + 1.0) / MOBA_HEADS)
    return jnp.asarray(slopes, F32)


def _moba_past_tiles(n_b):
    ceil_div = lambda a: -(-a // MOBA_GROUP)
    totals = {ceil_div(s) + ceil_div(n_b - 1 - s) for s in range(n_b // 2)}
    assert len(totals) == 1 and n_b % 2 == 0
    return totals.pop()


def _key_side_pattern():
    sp, win = MOBA_SPLIT, MOBA_WINDOW
    assert 2 * sp <= 8 and MOBA_AUX_ROWS <= MOBA_KEY_LANES
    pos = np.arange(win * MOBA_BLOCK)
    pat = np.zeros((pos.size, MOBA_KEY_LANES), np.float32)
    pat[:, :sp] = (pos % MOBA_BLOCK)[:, None]
    pat[:, sp:2 * sp] = 1.0
    for s in range(sp):
        for c in range(win):
            pat[:, 8 + s * win + c] = pos // MOBA_BLOCK == c
    return jnp.asarray(pat, BF16)


def _moba_kernel(slope_ref, qa_ref, qb_ref, k_ref, kpat_ref, vt_ref, oa_ref, ob_ref,
                 kmean_ref, qtop_ref, aux_ref, m_ref, acc_ref):
    blk, hd, grp = MOBA_BLOCK, MOBA_HEAD_DIM, MOBA_GROUP
    kl, aux_rows, win, sp = MOBA_KEY_LANES, MOBA_AUX_ROWS, MOBA_WINDOW, MOBA_SPLIT
    assert k_ref.shape[2] == 2 * hd == kl
    tile = grp * blk
    n_b = vt_ref.shape[1]
    n_past = _moba_past_tiles(n_b)
    pair = pl.program_id(1)
    step = pl.program_id(2)
    own = (step, n_b - 1 - step)
    n_past_first = (step + grp - 1) // grp
    q_refs, o_refs = (qa_ref, qb_ref), (oa_ref, ob_ref)

    @pl.when(step == 0)
    def _():
        for b in range(n_b):
            kb = k_ref[0, b * blk:(b + 1) * blk, :].astype(F32)
            kmean_ref[b:b + 1, :] = jnp.sum(kb, axis=0, keepdims=True) * (1.0 / blk)

    cand_blocks = (n_b // 2, n_b)
    assert cand_blocks[0] % win == 0
    causal = (lax.broadcasted_iota(jnp.int32, (blk, blk), 0)
              <= lax.broadcasted_iota(jnp.int32, (blk, blk), 1))
    row8 = lax.broadcasted_iota(jnp.int32, (8, blk), 0)
    r_q8 = lax.broadcasted_iota(jnp.int32, (8, blk), 1).astype(F32)
    zero_rows = lambda n: jnp.zeros((n, blk), BF16)
    ones_rows = lambda n: (lax.broadcasted_iota(jnp.int32, (MOBA_SUM_ROWS, n), 0) == 0).astype(BF16)

    def bf16_pieces(v):
        pieces = []
        for _ in range(sp):
            piece = v.astype(BF16).astype(F32)
            pieces.append(piece)
            v = v - piece
        return pieces

    def col_max(u):
        return jnp.max(jnp.max(u.reshape(-1, blk, blk), axis=0), axis=0, keepdims=True)

    alibi_rows = []
    for hl in range(2):
        c = slope_ref[2 * pair + hl] * LOG2E
        pieces = bf16_pieces(jnp.where(row8 < sp, c, -c * r_q8))
        rows = jnp.zeros((8, blk), F32)
        for s in range(sp):
            rows = jnp.where((row8 == s) | (row8 == sp + s), pieces[s], rows)
        alibi_rows.append(rows)

    chains = [(o, hl) for o in range(2) for hl in range(2)]
    head_row = lax.broadcasted_iota(jnp.int32, (kl, blk), 0) // hd
    own_scores, gates = {}, {}
    for o, hl in chains:
        q_pair = q_refs[o][0, 0]
        q_head = jnp.where(head_row == hl, q_pair, jnp.zeros_like(q_pair))
        qtop_ref[o, hl] = q_head
        aux_own = jnp.concatenate([alibi_rows[hl], jnp.zeros((aux_rows - 8, blk), F32)], axis=0)
        q_own = jnp.concatenate([q_head, aux_own.astype(BF16), zero_rows(kl - aux_rows)], axis=0)
        kd = jnp.concatenate(
            [k_ref[0, pl.ds(pl.multiple_of(own[o] * blk, blk), blk), :],
             kpat_ref[pl.ds(pl.multiple_of((own[o] % win) * blk, blk), blk), :]], axis=1)
        own_scores[o, hl] = jnp.dot(kd, q_own, preferred_element_type=F32)
        kmean = kmean_ref[:cand_blocks[o], :].astype(BF16)
        gates[o, hl] = jnp.dot(kmean, q_head, preferred_element_type=F32)

    for o, hl in chains:
        u = jnp.where(causal, own_scores[o, hl], NEG_INF)
        m = col_max(u)
        p = jnp.exp2(u - m)
        m_ref[o, hl] = m
        vt = jnp.concatenate([vt_ref[0, own[o], hl * hd:(hl + 1) * hd, :], ones_rows(blk)], axis=0)
        acc_ref[o, hl] = jnp.dot(vt, p.astype(BF16), preferred_element_type=F32)

        jidx = lax.broadcasted_iota(jnp.int32, (cand_blocks[o], blk), 0)
        past = jidx < own[o]
        gate = gates[o, hl]
        taken = jnp.zeros(jidx.shape, jnp.bool_)
        for _ in range(MOBA_TOP_K):
            cand = past & jnp.logical_not(taken)
            best = jnp.max(jnp.where(cand, gate, -jnp.inf), axis=0, keepdims=True)
            pick = cand & (gate == best)
            first = jnp.min(jnp.where(pick, jidx, n_b), axis=0, keepdims=True)
            taken = taken | (jidx == first)
        c = slope_ref[2 * pair + hl] * LOG2E
        block_off = (own[o] - jidx).astype(F32) * (-float(blk) * c)
        bias_pieces = bf16_pieces(jnp.where(taken, block_off, NEG_INF))
        for w in range(cand_blocks[o] // win):
            rows = [alibi_rows[hl]] + [piece[w * win:(w + 1) * win, :] for piece in bias_pieces]
            aux_ref[o, hl, w] = jnp.concatenate(rows, axis=0).astype(BF16)

    tiles = []
    for k in range(n_past):
        which = jnp.where(k >= n_past_first, 1, 0)
        tiles.append((which, k - which * n_past_first))

    def unit_scores(k, hl):
        which, t = tiles[k]
        q_op = jnp.concatenate([qtop_ref[which, hl], aux_ref[which, hl, (t * grp) // win],
                                zero_rows(kl - aux_rows)], axis=0)
        k_rows = jnp.concatenate(
            [k_ref[0, pl.ds(pl.multiple_of(t * tile, tile), tile), :],
             kpat_ref[pl.ds(pl.multiple_of((t % (win // grp)) * tile, tile), tile), :]], axis=1)
        s = jnp.dot(k_rows, q_op, preferred_element_type=F32)
        return s, col_max(s)

    def unit_update(k, hl, s, cmax):
        which, t = tiles[k]
        m = m_ref[which, hl]
        m_new = jnp.maximum(m, cmax)
        alpha = jnp.exp2(m - m_new)
        p = jnp.exp2(s - m_new)
        m_ref[which, hl] = m_new
        vt = jnp.concatenate(
            [vt_ref[0, t * grp + c, hl * hd:(hl + 1) * hd, :] for c in range(grp)], axis=1)
        vt = jnp.concatenate([vt, ones_rows(tile)], axis=0)
        acc_ref[which, hl] = alpha * acc_ref[which, hl] + jnp.dot(
            vt, p.astype(BF16), preferred_element_type=F32)

    units = [(k, hl) for k in range(n_past) for hl in range(2)]
    pending = []
    for idx in range(len(units) + MOBA_LOOKAHEAD):
        if idx < len(units):
            pending.append(unit_scores(*units[idx]))
        if idx >= MOBA_LOOKAHEAD:
            unit_update(*units[idx - MOBA_LOOKAHEAD], *pending.pop(0))

    for o in range(2):
        outs = [acc_ref[o, hl, :hd, :] * (1.0 / acc_ref[o, hl, hd:hd + 1, :]) for hl in range(2)]
        o_refs[o][0] = jnp.concatenate(outs, axis=0).T.astype(oa_ref.dtype)


def _moba(mqt, mk, mvt, slopes):
    bsz, n_b, width, blk = mqt.shape
    s_len = mk.shape[1]
    pw = 2 * MOBA_HEAD_DIM
    tile = MOBA_GROUP * blk
    half = n_b // 2
    kpat = _key_side_pattern()
    est = 2 * s_len * 2 * pw * 2 + 2 * kpat.size * 2 + 4 * 2 * 4 * tile * blk * 4
    out_half = jax.ShapeDtypeStruct((bsz, half * blk, width), BF16)
    return pl.pallas_call(
        _moba_kernel,
        grid=(bsz, width // pw, half),
        in_specs=[
            pl.BlockSpec(memory_space=pltpu.SMEM),
            pl.BlockSpec((1, 1, pw, blk), lambda b, p, s: (b, s, p, 0)),
            pl.BlockSpec((1, 1, pw, blk), lambda b, p, s: (b, n_b - 1 - s, p, 0)),
            pl.BlockSpec((1, s_len, pw), lambda b, p, s: (b, 0, p)),
            pl.BlockSpec(kpat.shape, lambda b, p, s: (0, 0)),
            pl.BlockSpec((1, n_b, pw, blk), lambda b, p, s: (b, 0, p, 0)),
        ],
        out_specs=[pl.BlockSpec((1, blk, pw), lambda b, p, s: (b, s, p)),
                   pl.BlockSpec((1, blk, pw), lambda b, p, s: (b, half - 1 - s, p))],
        out_shape=[out_half, out_half],
        scratch_shapes=[pltpu.VMEM((n_b, pw), F32),
                        pltpu.VMEM((2, 2, pw, blk), BF16),
                        pltpu.VMEM((2, 2, n_b // MOBA_WINDOW, MOBA_AUX_ROWS, blk), BF16),
                        pltpu.VMEM((2, 2, 1, blk), F32),
                        pltpu.VMEM((2, 2, MOBA_HEAD_DIM + MOBA_SUM_ROWS, blk), F32)],
        compiler_params=pltpu.CompilerParams(
            dimension_semantics=("parallel", "parallel", "arbitrary"),
            vmem_limit_bytes=_vmem_limit(est)),
        name="moba",
    )(slopes, mqt, mqt, mk, kpat, mvt)


def _mix_ffn_kernel(alpha, lo_tiles, x_ref, ret_ref, mlo_ref, mhi_ref, wr_ref, wm_ref, g1_ref, b1_ref,
                    wg_ref, wu_ref, wd_ref, g2_ref, b2_ref, o_ref):
    moba = jnp.where(pl.program_id(1) < lo_tiles, mlo_ref[0], mhi_ref[0])
    dot = functools.partial(jnp.dot, preferred_element_type=F32)
    rows = x_ref.shape[1] // FFN_SUBTILES
    part = lambda a, i: a[i * rows:(i + 1) * rows]
    mixed = [dot(part(ret_ref[0], i), wr_ref[...]) + dot(part(moba, i), wm_ref[...])
             for i in range(FFN_SUBTILES)]
    h, act, ffn = {}, {}, {}
    for i in range(FFN_SUBTILES + 2):
        if i < FFN_SUBTILES:
            h[i] = _layer_norm(alpha * part(x_ref[0], i) + mixed[i], g1_ref[...], b1_ref[...])
            hb = h[i].astype(BF16)
            gate = dot(hb, wg_ref[...])
            up = dot(hb, wu_ref[...])
            act[i] = (gate * jax.nn.sigmoid(gate) * up).astype(BF16)
        if 1 <= i <= FFN_SUBTILES:
            ffn[i - 1] = dot(act[i - 1], wd_ref[...])
        if i >= 2:
            j = i - 2
            o_ref[0, j * rows:(j + 1) * rows, :] = _layer_norm(alpha * h[j] + ffn[j], g2_ref[...], b2_ref[...])


def _mix_ffn(x, ret, moba_lo, moba_hi, w_ret, w_moba, g1, b1, w_gate, w_up, w_down, g2, b2, alpha):
    bsz, s_len, d = x.shape
    d_ff = w_gate.shape[1]
    tm = FFN_TOKEN_TILE
    sub = tm // FFN_SUBTILES
    lo_tiles = moba_lo.shape[1] // tm
    assert moba_lo.shape[1] % tm == 0 and moba_hi.shape[1] == s_len - moba_lo.shape[1]
    row = lambda w: pl.BlockSpec((1, tm, w), lambda b_, t: (b_, t, 0))
    lo_row = pl.BlockSpec((1, tm, MOBA_WIDTH), lambda b_, t: (b_, jnp.minimum(t, lo_tiles - 1), 0))
    hi_row = pl.BlockSpec((1, tm, MOBA_WIDTH), lambda b_, t: (b_, jnp.maximum(t - lo_tiles, 0), 0))
    full = lambda a: pl.BlockSpec(a.shape, lambda b_, t: (0, 0), pipeline_mode=pl.Buffered(1))
    est = ((3 * d * d_ff + d * d) * 2 + 2 * 2 * tm * d * 4 + 2 * 3 * tm * RET_WIDTH * 2
           + 2 * sub * d_ff * (4 + 4 + 2) + FFN_SUBTILES * sub * d * 4 * 2)
    return pl.pallas_call(
        functools.partial(_mix_ffn_kernel, alpha, lo_tiles),
        grid=(bsz, s_len // tm),
        in_specs=[row(d), row(RET_WIDTH), lo_row, hi_row, full(w_ret), full(w_moba), full(g1), full(b1),
                  full(w_gate), full(w_up), full(w_down), full(g2), full(b2)],
        out_specs=row(d),
        out_shape=jax.ShapeDtypeStruct((bsz, s_len, d), F32),
        compiler_params=pltpu.CompilerParams(
            dimension_semantics=("parallel", "parallel"),
            vmem_limit_bytes=_vmem_limit(est)),
        name="mix_ffn",
    )(x, ret, moba_lo, moba_hi, w_ret, w_moba, g1, b1, w_gate, w_up, w_down, g2, b2)


def kernel(x, w_in, ret_gn_gain, w_out, ln1_g, ln1_b, w_gate, w_up, w_down, ln2_g, ln2_b):
    bsz, s_len, d = x.shape
    depth = w_in.shape[0]
    assert s_len % math.lcm(RET_CHUNK, MOBA_BLOCK, TOKEN_TILE, FFN_TOKEN_TILE) == 0
    assert w_in.shape[2] == 4 * RET_WIDTH + 3 * MOBA_WIDTH
    alpha = (2 * depth) ** 0.25
    ret_tables = _retention_tables()
    slopes = _alibi_slopes()
    r4 = 4 * RET_WIDTH
    h = x
    for layer in range(depth):
        w = w_in[layer]
        w_nat = jnp.concatenate([w[:, :r4], w[:, r4 + MOBA_WIDTH:r4 + 2 * MOBA_WIDTH]], axis=1).astype(BF16)
        w_tr = jnp.concatenate([w[:, r4:r4 + MOBA_WIDTH], w[:, r4 + 2 * MOBA_WIDTH:]], axis=1).T.astype(BF16)
        rqkv, rg, mk, mqt, mvt = _inproj(h, w_nat, w_tr)
        ret = _retention(rqkv, rg, ret_gn_gain[layer][None, :], ret_tables)
        moba_lo, moba_hi = _moba(mqt, mk, mvt, slopes)
        wo = w_out[layer].astype(BF16)
        h = _mix_ffn(h, ret, moba_lo, moba_hi, wo[:RET_WIDTH], wo[RET_WIDTH:],
                     ln1_g[layer][None, :], ln1_b[layer][None, :],
                     w_gate[layer].astype(BF16), w_up[layer].astype(BF16), w_down[layer].astype(BF16),
                     ln2_g[layer][None, :], ln2_b[layer][None, :], alpha)
    return h
```

```python
import functools
import math

import numpy as np
import jax
import jax.numpy as jnp
from jax import lax
from jax.experimental import pallas as pl
from jax.experimental.pallas import tpu as pltpu

F32 = jnp.float32
BF16 = jnp.bfloat16

V7X_VMEM_BYTES = 64 * 1024 * 1024
LANES = 128

RET_HEADS = 4
RET_HEAD_DIM = 128
RET_WIDTH = RET_HEADS * RET_HEAD_DIM
RET_CHUNK = 256
RET_CHUNKS_PER_STEP = 4
MOBA_HEADS = 8
MOBA_HEAD_DIM = 64
MOBA_WIDTH = MOBA_HEADS * MOBA_HEAD_DIM
MOBA_BLOCK = 256
MOBA_TOP_K = 3
MOBA_GROUP = 1
MOBA_LOOKAHEAD = 5
MOBA_SUM_ROWS = 16
MOBA_KEY_LANES = LANES
MOBA_WINDOW = 8
MOBA_SPLIT = 3
MOBA_AUX_ROWS = 8 + MOBA_SPLIT * MOBA_WINDOW
LOG2E = 1.4426950408889634
NORM_EPS = 1e-5
NEG_INF = -1e30

TOKEN_TILE = 1024
FFN_TOKEN_TILE = 1024
FFN_SUBTILES = 4


def _vmem_limit(estimate_bytes):
    return int(min(V7X_VMEM_BYTES - 4 * 1024 * 1024, max(estimate_bytes, 16 * 1024 * 1024)))


def _nt_dot(a, b):
    return lax.dot_general(a, b, (((1,), (1,)), ((), ())), preferred_element_type=F32)


def _layer_norm(v, g, b):
    mu = jnp.mean(v, axis=-1, keepdims=True)
    d = v - mu
    var = jnp.mean(d * d, axis=-1, keepdims=True)
    return d * lax.rsqrt(var + NORM_EPS) * g + b


def _inproj_kernel(cd_ref, x_ref, wn_ref, wt_ref, gain_ref, dec_ref, kte_ref, qfs_ref,
                   ret_ref, mk_ref, mqt_ref, mvt_ref, rqkv_ref, rg_ref, state_ref):
    x = x_ref[0].astype(BF16)
    nat = jnp.dot(x, wn_ref[...], preferred_element_type=F32)
    rqkv_ref[0] = nat[:, :3 * RET_WIDTH].astype(BF16)
    rg_ref[0] = nat[:, 3 * RET_WIDTH:4 * RET_WIDTH]
    mk_ref[0] = nat[:, 4 * RET_WIDTH:].astype(BF16)
    assert RET_CHUNK == MOBA_BLOCK

    def transposed_block(c):
        tr = _nt_dot(wt_ref[...], x[c * MOBA_BLOCK:(c + 1) * MOBA_BLOCK])
        mqt_ref[0, c] = (tr[:MOBA_WIDTH] * (MOBA_HEAD_DIM ** -0.5 * LOG2E)).astype(BF16)
        mvt_ref[0, c] = tr[MOBA_WIDTH:].astype(BF16)

    side_work = [functools.partial(transposed_block, c) for c in range(x.shape[0] // MOBA_BLOCK)]
    _retention_chunks(cd_ref, rqkv_ref, rg_ref, gain_ref, dec_ref, kte_ref, qfs_ref, ret_ref, state_ref,
                      side_work)


def _inproj(x, w_nat, w_tr, gain, tables):
    bsz, s_len, d = x.shape
    tm = TOKEN_TILE
    assert tm == RET_CHUNK * RET_CHUNKS_PER_STEP
    nb = s_len // MOBA_BLOCK
    n_nat = w_nat.shape[1]
    dec, kte, qfs, cd = tables
    once = lambda a: pl.BlockSpec(a.shape, lambda b, t: (0,) * a.ndim, pipeline_mode=pl.Buffered(1))
    est = (2 * tm * d * 4 + (w_nat.size + w_tr.size) * 2 + (dec.size + kte.size + qfs.size) * 4
           + 2 * tm * (RET_WIDTH * 2 + 3 * MOBA_WIDTH * 2) + tm * (3 * RET_WIDTH * 2 + RET_WIDTH * 4)
           + tm * (n_nat + 2 * MOBA_WIDTH) * 4 * 2)
    return pl.pallas_call(
        _inproj_kernel,
        grid=(bsz, s_len // tm),
        in_specs=[
            pl.BlockSpec(memory_space=pltpu.SMEM),
            pl.BlockSpec((1, tm, d), lambda b, t: (b, t, 0)),
            once(w_nat), once(w_tr), once(gain), once(dec), once(kte), once(qfs),
        ],
        out_specs=[
            pl.BlockSpec((1, tm, RET_WIDTH), lambda b, t: (b, t, 0)),
            pl.BlockSpec((1, tm, MOBA_WIDTH), lambda b, t: (b, t, 0)),
            pl.BlockSpec((1, tm // MOBA_BLOCK, MOBA_WIDTH, MOBA_BLOCK), lambda b, t: (b, t, 0, 0)),
            pl.BlockSpec((1, tm // MOBA_BLOCK, MOBA_WIDTH, MOBA_BLOCK), lambda b, t: (b, t, 0, 0)),
        ],
        out_shape=[
            jax.ShapeDtypeStruct((bsz, s_len, RET_WIDTH), BF16),
            jax.ShapeDtypeStruct((bsz, s_len, MOBA_WIDTH), BF16),
            jax.ShapeDtypeStruct((bsz, nb, MOBA_WIDTH, MOBA_BLOCK), BF16),
            jax.ShapeDtypeStruct((bsz, nb, MOBA_WIDTH, MOBA_BLOCK), BF16),
        ],
        scratch_shapes=[pltpu.VMEM((1, tm, 3 * RET_WIDTH), BF16),
                        pltpu.VMEM((1, tm, RET_WIDTH), F32),
                        pltpu.VMEM((RET_HEADS, RET_HEAD_DIM, RET_HEAD_DIM), F32)],
        compiler_params=pltpu.CompilerParams(
            dimension_semantics=("parallel", "arbitrary"),
            vmem_limit_bytes=_vmem_limit(est)),
        name="inproj_retention",
    )(cd, x, w_nat, w_tr, gain, dec, kte, qfs)


def _retention_tables():
    h = np.arange(RET_HEADS, dtype=np.float64)
    log_g = np.log1p(-np.exp2(-5.0 - h))
    pos = np.arange(RET_CHUNK, dtype=np.float64)
    diff = pos[:, None] - pos[None, :]
    scale = RET_HEAD_DIM ** -0.5
    intra = np.where(diff >= 0.0, np.exp(log_g[:, None, None] * np.maximum(diff, 0.0)), 0.0) * scale
    k_to_end = np.exp(log_g[:, None] * (RET_CHUNK - 1.0 - pos)[None, :]) * scale
    q_from_start = np.exp(log_g[:, None] * (pos + 1.0)[None, :])
    chunk_decay = np.exp(log_g * RET_CHUNK)
    bcast = lambda t: np.broadcast_to(t[:, :, None], (RET_HEADS, RET_CHUNK, RET_HEAD_DIM))
    return (jnp.asarray(intra, F32), jnp.asarray(bcast(k_to_end), F32),
            jnp.asarray(bcast(q_from_start), F32), jnp.asarray(chunk_decay, F32))


def _retention_chunks(cd_ref, qkv_ref, rg_ref, gain_ref, dec_ref, kte_ref, qfs_ref, o_ref, state_ref, side_work):
    e, c = RET_HEAD_DIM, RET_CHUNK

    @pl.when(pl.program_id(1) == 0)
    def _():
        state_ref[...] = jnp.zeros_like(state_ref)

    heads = range(RET_HEADS)
    cols = lambda part, h: slice((part * RET_HEADS + h) * e, (part * RET_HEADS + h + 1) * e)
    state = [state_ref[h] for h in heads]
    for j in range(RET_CHUNKS_PER_STEP):
        side_work[j]()
        rows = slice(j * c, (j + 1) * c)
        q = [qkv_ref[0, rows, cols(0, h)] for h in heads]
        k = [qkv_ref[0, rows, cols(1, h)] for h in heads]
        v = [qkv_ref[0, rows, cols(2, h)] for h in heads]
        raw = [_nt_dot(q[h], k[h]) for h in heads]
        cross = [jnp.dot(q[h], state[h].astype(BF16), preferred_element_type=F32) for h in heads]
        scores = [(raw[h] * dec_ref[h]).astype(BF16) for h in heads]
        k_dec = [(k[h].astype(F32) * kte_ref[h]).T.astype(BF16) for h in heads]
        intra = [jnp.dot(scores[h], v[h], preferred_element_type=F32) for h in heads]
        kv = [jnp.dot(k_dec[h], v[h], preferred_element_type=F32) for h in heads]
        state = [state[h] * cd_ref[h] + kv[h] for h in heads]
        for h in heads:
            y = intra[h] + cross[h] * qfs_ref[h]
            mu = jnp.mean(y, axis=-1, keepdims=True)
            d = y - mu
            var = jnp.mean(d * d, axis=-1, keepdims=True)
            yn = d * lax.rsqrt(var + NORM_EPS) * gain_ref[:, h * e:(h + 1) * e]
            g = rg_ref[0, rows, h * e:(h + 1) * e]
            o_ref[0, rows, h * e:(h + 1) * e] = (g * jax.nn.sigmoid(g) * yn).astype(o_ref.dtype)
    for h in heads:
        state_ref[h] = state[h]


def _alibi_slopes():
    slopes = np.exp2(-8.0 * (np.arange(MOBA_HEADS, dtype=np.float64) + 1.0) / MOBA_HEADS)
    return jnp.asarray(slopes, F32)


def _moba_past_tiles(n_b):
    ceil_div = lambda a: -(-a // MOBA_GROUP)
    totals = {ceil_div(s) + ceil_div(n_b - 1 - s) for s in range(n_b // 2)}
    assert len(totals) == 1 and n_b % 2 == 0
    return totals.pop()


def _key_side_pattern():
    sp, win = MOBA_SPLIT, MOBA_WINDOW
    assert 2 * sp <= 8 and MOBA_AUX_ROWS <= MOBA_KEY_LANES
    pos = np.arange(win * MOBA_BLOCK)
    pat = np.zeros((pos.size, MOBA_KEY_LANES), np.float32)
    pat[:, :sp] = (pos % MOBA_BLOCK)[:, None]
    pat[:, sp:2 * sp] = 1.0
    for s in range(sp):
        for c in range(win):
            pat[:, 8 + s * win + c] = pos // MOBA_BLOCK == c
    return jnp.asarray(pat, BF16)


def _moba_kernel(slope_ref, qa_ref, qb_ref, k_ref, kpat_ref, vt_ref, oa_ref, ob_ref,
                 kmean_ref, qtop_ref, aux_ref, m_ref, acc_ref):
    blk, hd, grp = MOBA_BLOCK, MOBA_HEAD_DIM, MOBA_GROUP
    kl, aux_rows, win, sp = MOBA_KEY_LANES, MOBA_AUX_ROWS, MOBA_WINDOW, MOBA_SPLIT
    assert k_ref.shape[2] == 2 * hd == kl
    tile = grp * blk
    n_b = vt_ref.shape[1]
    n_past = _moba_past_tiles(n_b)
    pair = pl.program_id(1)
    step = pl.program_id(2)
    own = (step, n_b - 1 - step)
    n_past_first = (step + grp - 1) // grp
    q_refs, o_refs = (qa_ref, qb_ref), (oa_ref, ob_ref)

    @pl.when(step == 0)
    def _():
        for b in range(n_b):
            kb = k_ref[0, b * blk:(b + 1) * blk, :].astype(F32)
            kmean_ref[b:b + 1, :] = jnp.sum(kb, axis=0, keepdims=True) * (1.0 / blk)

    cand_blocks = (n_b // 2, n_b)
    assert cand_blocks[0] % win == 0
    causal = (lax.broadcasted_iota(jnp.int32, (blk, blk), 0)
              <= lax.broadcasted_iota(jnp.int32, (blk, blk), 1))
    row8 = lax.broadcasted_iota(jnp.int32, (8, blk), 0)
    r_q8 = lax.broadcasted_iota(jnp.int32, (8, blk), 1).astype(F32)
    zero_rows = lambda n: jnp.zeros((n, blk), BF16)
    ones_rows = lambda n: (lax.broadcasted_iota(jnp.int32, (MOBA_SUM_ROWS, n), 0) == 0).astype(BF16)

    def bf16_pieces(v):
        pieces = []
        for _ in range(sp):
            piece = v.astype(BF16).astype(F32)
            pieces.append(piece)
            v = v - piece
        return pieces

    def col_max(u):
        return jnp.max(jnp.max(u.reshape(-1, blk, blk), axis=0), axis=0, keepdims=True)

    alibi_rows = []
    for hl in range(2):
        c = slope_ref[2 * pair + hl] * LOG2E
        pieces = bf16_pieces(jnp.where(row8 < sp, c, -c * r_q8))
        rows = jnp.zeros((8, blk), F32)
        for s in range(sp):
            rows = jnp.where((row8 == s) | (row8 == sp + s), pieces[s], rows)
        alibi_rows.append(rows)

    chains = [(o, hl) for o in range(2) for hl in range(2)]
    head_row = lax.broadcasted_iota(jnp.int32, (kl, blk), 0) // hd
    own_scores, gates = {}, {}
    for o, hl in chains:
        q_pair = q_refs[o][0, 0]
        q_head = jnp.where(head_row == hl, q_pair, jnp.zeros_like(q_pair))
        qtop_ref[o, hl] = q_head
        aux_own = jnp.concatenate([alibi_rows[hl], jnp.zeros((aux_rows - 8, blk), F32)], axis=0)
        q_own = jnp.concatenate([q_head, aux_own.astype(BF16), zero_rows(kl - aux_rows)], axis=0)
        kd = jnp.concatenate(
            [k_ref[0, pl.ds(pl.multiple_of(own[o] * blk, blk), blk), :],
             kpat_ref[pl.ds(pl.multiple_of((own[o] % win) * blk, blk), blk), :]], axis=1)
        own_scores[o, hl] = jnp.dot(kd, q_own, preferred_element_type=F32)
        kmean = kmean_ref[:cand_blocks[o], :].astype(BF16)
        gates[o, hl] = jnp.dot(kmean, q_head, preferred_element_type=F32)

    for o, hl in chains:
        u = jnp.where(causal, own_scores[o, hl], NEG_INF)
        m = col_max(u)
        p = jnp.exp2(u - m)
        m_ref[o, hl] = m
        vt = jnp.concatenate([vt_ref[0, own[o], hl * hd:(hl + 1) * hd, :], ones_rows(blk)], axis=0)
        acc_ref[o, hl] = jnp.dot(vt, p.astype(BF16), preferred_element_type=F32)

        jidx = lax.broadcasted_iota(jnp.int32, (cand_blocks[o], blk), 0)
        past = jidx < own[o]
        gate = gates[o, hl]
        taken = jnp.zeros(jidx.shape, jnp.bool_)
        for _ in range(MOBA_TOP_K):
            cand = past & jnp.logical_not(taken)
            best = jnp.max(jnp.where(cand, gate, -jnp.inf), axis=0, keepdims=True)
            pick = cand & (gate == best)
            first = jnp.min(jnp.where(pick, jidx, n_b), axis=0, keepdims=True)
            taken = taken | (jidx == first)
        c = slope_ref[2 * pair + hl] * LOG2E
        block_off = (own[o] - jidx).astype(F32) * (-float(blk) * c)
        bias_pieces = bf16_pieces(jnp.where(taken, block_off, NEG_INF))
        for w in range(cand_blocks[o] // win):
            rows = [alibi_rows[hl]] + [piece[w * win:(w + 1) * win, :] for piece in bias_pieces]
            aux_ref[o, hl, w] = jnp.concatenate(rows, axis=0).astype(BF16)

    tiles = []
    for k in range(n_past):
        which = jnp.where(k >= n_past_first, 1, 0)
        tiles.append((which, k - which * n_past_first))

    def unit_scores(k, hl):
        which, t = tiles[k]
        q_op = jnp.concatenate([qtop_ref[which, hl], aux_ref[which, hl, (t * grp) // win],
                                zero_rows(kl - aux_rows)], axis=0)
        k_rows = jnp.concatenate(
            [k_ref[0, pl.ds(pl.multiple_of(t * tile, tile), tile), :],
             kpat_ref[pl.ds(pl.multiple_of((t % (win // grp)) * tile, tile), tile), :]], axis=1)
        s = jnp.dot(k_rows, q_op, preferred_element_type=F32)
        return s, col_max(s)

    def unit_update(k, hl, s, cmax):
        which, t = tiles[k]
        m = m_ref[which, hl]
        m_new = jnp.maximum(m, cmax)
        alpha = jnp.exp2(m - m_new)
        p = jnp.exp2(s - m_new)
        m_ref[which, hl] = m_new
        vt = jnp.concatenate(
            [vt_ref[0, t * grp + c, hl * hd:(hl + 1) * hd, :] for c in range(grp)], axis=1)
        vt = jnp.concatenate([vt, ones_rows(tile)], axis=0)
        acc_ref[which, hl] = alpha * acc_ref[which, hl] + jnp.dot(
            vt, p.astype(BF16), preferred_element_type=F32)

    units = [(k, hl) for k in range(n_past) for hl in range(2)]
    pending = []
    for idx in range(len(units) + MOBA_LOOKAHEAD):
        if idx < len(units):
            pending.append(unit_scores(*units[idx]))
        if idx >= MOBA_LOOKAHEAD:
            unit_update(*units[idx - MOBA_LOOKAHEAD], *pending.pop(0))

    for o in range(2):
        outs = [acc_ref[o, hl, :hd, :] * (1.0 / acc_ref[o, hl, hd:hd + 1, :]) for hl in range(2)]
        o_refs[o][0] = jnp.concatenate(outs, axis=0).T.astype(oa_ref.dtype)


def _moba(mqt, mk, mvt, slopes):
    bsz, n_b, width, blk = mqt.shape
    s_len = mk.shape[1]
    pw = 2 * MOBA_HEAD_DIM
    tile = MOBA_GROUP * blk
    half = n_b // 2
    kpat = _key_side_pattern()
    est = 2 * s_len * 2 * pw * 2 + 2 * kpat.size * 2 + 4 * 2 * 4 * tile * blk * 4
    out_half = jax.ShapeDtypeStruct((bsz, half * blk, width), BF16)
    return pl.pallas_call(
        _moba_kernel,
        grid=(bsz, width // pw, half),
        in_specs=[
            pl.BlockSpec(memory_space=pltpu.SMEM),
            pl.BlockSpec((1, 1, pw, blk), lambda b, p, s: (b, s, p, 0)),
            pl.BlockSpec((1, 1, pw, blk), lambda b, p, s: (b, n_b - 1 - s, p, 0)),
            pl.BlockSpec((1, s_len, pw), lambda b, p, s: (b, 0, p)),
            pl.BlockSpec(kpat.shape, lambda b, p, s: (0, 0)),
            pl.BlockSpec((1, n_b, pw, blk), lambda b, p, s: (b, 0, p, 0)),
        ],
        out_specs=[pl.BlockSpec((1, blk, pw), lambda b, p, s: (b, s, p)),
                   pl.BlockSpec((1, blk, pw), lambda b, p, s: (b, half - 1 - s, p))],
        out_shape=[out_half, out_half],
        scratch_shapes=[pltpu.VMEM((n_b, pw), F32),
                        pltpu.VMEM((2, 2, pw, blk), BF16),
                        pltpu.VMEM((2, 2, n_b // MOBA_WINDOW, MOBA_AUX_ROWS, blk), BF16),
                        pltpu.VMEM((2, 2, 1, blk), F32),
                        pltpu.VMEM((2, 2, MOBA_HEAD_DIM + MOBA_SUM_ROWS, blk), F32)],
        compiler_params=pltpu.CompilerParams(
            dimension_semantics=("parallel", "parallel", "arbitrary"),
            vmem_limit_bytes=_vmem_limit(est)),
        name="moba",
    )(slopes, mqt, mqt, mk, kpat, mvt)


def _mix_ffn_kernel(alpha, lo_tiles, x_ref, ret_ref, mlo_ref, mhi_ref, wr_ref, wm_ref, g1_ref, b1_ref,
                    wg_ref, wu_ref, wd_ref, g2_ref, b2_ref, o_ref):
    moba = jnp.where(pl.program_id(1) < lo_tiles, mlo_ref[0], mhi_ref[0])
    dot = functools.partial(jnp.dot, preferred_element_type=F32)
    rows = x_ref.shape[1] // FFN_SUBTILES
    part = lambda a, i: a[i * rows:(i + 1) * rows]
    mixed = [dot(part(ret_ref[0], i), wr_ref[...]) + dot(part(moba, i), wm_ref[...])
             for i in range(FFN_SUBTILES)]
    h, act, ffn = {}, {}, {}
    for i in range(FFN_SUBTILES + 2):
        if i < FFN_SUBTILES:
            h[i] = _layer_norm(alpha * part(x_ref[0], i) + mixed[i], g1_ref[...], b1_ref[...])
            hb = h[i].astype(BF16)
            gate = dot(hb, wg_ref[...])
            up = dot(hb, wu_ref[...])
            act[i] = (gate * jax.nn.sigmoid(gate) * up).astype(BF16)
        if 1 <= i <= FFN_SUBTILES:
            ffn[i - 1] = dot(act[i - 1], wd_ref[...])
        if i >= 2:
            j = i - 2
            o_ref[0, j * rows:(j + 1) * rows, :] = _layer_norm(alpha * h[j] + ffn[j], g2_ref[...], b2_ref[...])


def _mix_ffn(x, ret, moba_lo, moba_hi, w_ret, w_moba, g1, b1, w_gate, w_up, w_down, g2, b2, alpha):
    bsz, s_len, d = x.shape
    d_ff = w_gate.shape[1]
    tm = FFN_TOKEN_TILE
    sub = tm // FFN_SUBTILES
    lo_tiles = moba_lo.shape[1] // tm
    assert moba_lo.shape[1] % tm == 0 and moba_hi.shape[1] == s_len - moba_lo.shape[1]
    row = lambda w: pl.BlockSpec((1, tm, w), lambda b_, t: (b_, t, 0))
    lo_row = pl.BlockSpec((1, tm, MOBA_WIDTH), lambda b_, t: (b_, jnp.minimum(t, lo_tiles - 1), 0))
    hi_row = pl.BlockSpec((1, tm, MOBA_WIDTH), lambda b_, t: (b_, jnp.maximum(t - lo_tiles, 0), 0))
    full = lambda a: pl.BlockSpec(a.shape, lambda b_, t: (0, 0), pipeline_mode=pl.Buffered(1))
    est = ((3 * d * d_ff + d * d) * 2 + 2 * 2 * tm * d * 4 + 2 * 3 * tm * RET_WIDTH * 2
           + 2 * sub * d_ff * (4 + 4 + 2) + FFN_SUBTILES * sub * d * 4 * 2)
    return pl.pallas_call(
        functools.partial(_mix_ffn_kernel, alpha, lo_tiles),
        grid=(bsz, s_len // tm),
        in_specs=[row(d), row(RET_WIDTH), lo_row, hi_row, full(w_ret), full(w_moba), full(g1), full(b1),
                  full(w_gate), full(w_up), full(w_down), full(g2), full(b2)],
        out_specs=row(d),
        out_shape=jax.ShapeDtypeStruct((bsz, s_len, d), F32),
        compiler_params=pltpu.CompilerParams(
            dimension_semantics=("parallel", "parallel"),
            vmem_limit_bytes=_vmem_limit(est)),
        name="mix_ffn",
    )(x, ret, moba_lo, moba_hi, w_ret, w_moba, g1, b1, w_gate, w_up, w_down, g2, b2)


def kernel(x, w_in, ret_gn_gain, w_out, ln1_g, ln1_b, w_gate, w_up, w_down, ln2_g, ln2_b):
    bsz, s_len, d = x.shape
    depth = w_in.shape[0]
    assert s_len % math.lcm(RET_CHUNK, MOBA_BLOCK, TOKEN_TILE, FFN_TOKEN_TILE) == 0
    assert w_in.shape[2] == 4 * RET_WIDTH + 3 * MOBA_WIDTH
    alpha = (2 * depth) ** 0.25
    ret_tables = _retention_tables()
    slopes = _alibi_slopes()
    r4 = 4 * RET_WIDTH
    h = x
    for layer in range(depth):
        w = w_in[layer]
        w_nat = jnp.concatenate([w[:, :r4], w[:, r4 + MOBA_WIDTH:r4 + 2 * MOBA_WIDTH]], axis=1).astype(BF16)
        w_tr = jnp.concatenate([w[:, r4:r4 + MOBA_WIDTH], w[:, r4 + 2 * MOBA_WIDTH:]], axis=1).T.astype(BF16)
        ret, mk, mqt, mvt = _inproj(h, w_nat, w_tr, ret_gn_gain[layer][None, :], ret_tables)
        moba_lo, moba_hi = _moba(mqt, mk, mvt, slopes)
        wo = w_out[layer].astype(BF16)
        h = _mix_ffn(h, ret, moba_lo, moba_hi, wo[:RET_WIDTH], wo[RET_WIDTH:],
                     ln1_g[layer][None, :], ln1_b[layer][None, :],
                     w_gate[layer].astype(BF16), w_up[layer].astype(BF16), w_down[layer].astype(BF16),
                     ln2_g[layer][None, :], ln2_b[layer][None, :], alpha)
    return h
```
